```python
import math
import jax, jax.numpy as jnp
from jax import lax
import numpy as np

D_MODEL = 1024
BATCH = 8
SEQ = 2048
DEPTH = 4
DEC_BATCH = 128
DEC_SEQ = 8
PAST_LEN = 16384
PAGE_SIZE = 128

N_MIXERS = 2
N_S5 = (DEPTH + 1) // 2
N_RW = DEPTH // 2
S5_GROUP = 16
S5_GROUPS = D_MODEL // S5_GROUP
S5_STATE = 64
LOG_STEP_MIN = math.log(1e-3)
LOG_STEP_MAX = math.log(1e-1)
HEAD_DIM = 64
RW_HEADS = D_MODEL // HEAD_DIM
DECAY_LORA = 64
AAA_LORA = 64
MV_LORA = 32
GATE_LORA = 160
GN_EPS = 64e-5
N_SHIFT_MIX = 6
N_EXPERTS = 16
N_EXPERT_GROUPS = 4
EXPERTS_PER_GROUP = N_EXPERTS // N_EXPERT_GROUPS
TOP_K = 2
D_FF_EXPERT = D_MODEL
LN_EPS = 1e-5
ALPHA = (2.0 * DEPTH) ** 0.25
BETA = (8.0 * DEPTH) ** -0.25

kernel_name = 'hybrid_s5_rwkv7_grouped_moe_deepnorm_step'

f32 = jnp.float32


def _layer_norm(x, g, b):
    mu = jnp.mean(x, axis=-1, keepdims=True)
    var = jnp.mean(jnp.square(x - mu), axis=-1, keepdims=True)
    return (x - mu) * lax.rsqrt(var + LN_EPS) * g.astype(f32) + b.astype(f32)


def _post_norm(x, f, g, b):
    return _layer_norm(ALPHA * x.astype(f32) + f.astype(f32), g, b).astype(x.dtype)


def _ssm_combine(left, right):
    a_l, b_l = left
    a_r, b_r = right
    return a_r * a_l, a_r * b_l + b_r


def _s5_mixer(u, h0_re, h0_im, lam_re, lam_im, log_step, b_re, b_im, c_re, c_im, d_skip, w_out, w_gate):
    bsz, seq, _ = u.shape
    u32 = u.astype(f32)
    lam = lax.complex(lam_re.astype(f32), lam_im.astype(f32))
    step = jnp.exp(log_step.astype(f32))[:, None]
    a_bar = jnp.exp(lam * step)
    b_bar = ((a_bar - 1.0) / lam)[..., None] * lax.complex(b_re.astype(f32), b_im.astype(f32))
    c = lax.complex(c_re.astype(f32), c_im.astype(f32))
    ug = u32.reshape(bsz, seq, S5_GROUPS, S5_GROUP)
    bu = jnp.einsum('blgc,gpc->blgp', ug.astype(jnp.complex64), b_bar)
    a_seq = jnp.broadcast_to(a_bar, bu.shape)
    a_cum, h = lax.associative_scan(_ssm_combine, (a_seq, bu), axis=1)
    h = h + a_cum * lax.complex(h0_re.astype(f32), h0_im.astype(f32))[:, None]
    y = jnp.einsum('blgp,gcp->blgc', h, c).real.reshape(bsz, seq, D_MODEL) + d_skip.astype(f32) * u32
    z = jax.nn.gelu(y)
    out = (z @ w_out.astype(f32)) * jax.nn.sigmoid(z @ w_gate.astype(f32))
    h_last = h[:, -1]
    return out, h_last.real, h_last.imag


def _heads(t):
    return t.reshape(t.shape[0], t.shape[1], RW_HEADS, HEAD_DIM)


def _wkv_step(state, inp):
    r_t, w_t, k_t, v_t, kk_t, a_t = inp
    sa = jnp.einsum('bhvk,bhk->bhv', state, -kk_t)
    state = (state * w_t[:, :, None, :]
             + sa[..., None] * (kk_t * a_t)[:, :, None, :]
             + v_t[..., None] * k_t[:, :, None, :])
    out = jnp.einsum('bhvk,bhk->bhv', state, r_t)
    return state, out


def _rwkv7_mixer(x, shift0, s0, v_first, vres, mix, w_r, w_k, w_v, w_o, w0, w1, w2,
                 a0, a1, a2, g1, g2, k_k, k_a, r_k, lnx_g, lnx_b):
    bsz, seq, _ = x.shape
    x32 = x.astype(f32)
    prev = jnp.concatenate([shift0[:, None].astype(f32), x32[:, :-1]], axis=1)
    xx = prev - x32
    mix = mix.astype(f32)
    xr, xw, xk, xv, xa, xg = [x32 + xx * mix[i] for i in range(N_SHIFT_MIX)]
    r = xr @ w_r
    k = xk @ w_k
    v = xv @ w_v
    w = -jax.nn.softplus(-(w0 + jnp.tanh(xw @ w1) @ w2)) - 0.5
    decay = jnp.exp(-jnp.exp(w.astype(f32)))
    a = jax.nn.sigmoid(a0 + (xa @ a1) @ a2)
    g = jax.nn.sigmoid(xg @ g1) @ g2
    kk = _heads(k * k_k)
    kk = kk / jnp.maximum(jnp.sqrt(jnp.sum(kk * kk, axis=-1, keepdims=True)), 1e-12)
    k = k * (1.0 + (a - 1.0) * k_a)
    if vres is None:
        v_first = v
    else:
        v0, v1, v2 = vres
        v = v + (v_first - v) * jax.nn.sigmoid(v0 + (xv @ v1) @ v2)
    xs = tuple(jnp.moveaxis(t, 1, 0) for t in
               (_heads(r), _heads(decay), _heads(k), _heads(v), kk, _heads(a)))
    s_final, o = lax.scan(_wkv_step, s0.astype(f32), xs)
    o = jnp.moveaxis(o, 0, 1)
    mu = jnp.mean(o, axis=-1, keepdims=True)
    var = jnp.mean(jnp.square(o - mu), axis=-1, keepdims=True)
    o = ((o - mu) * lax.rsqrt(var + GN_EPS)).reshape(bsz, seq, D_MODEL) * lnx_g + lnx_b
    bonus = jnp.sum(_heads(r) * _heads(k) * r_k, axis=-1, keepdims=True) * _heads(v)
    o = o + bonus.reshape(bsz, seq, D_MODEL)
    out = (o * g) @ w_o
    return out, x32[:, -1], s_final, v_first


def _moe(x, w_router, router_bias, wg, wu, wd):
    bsz, seq, _ = x.shape
    t = x.reshape(-1, D_MODEL).astype(f32)
    probs = jax.nn.softmax(t @ w_router.astype(f32), axis=-1)
    sel = probs + router_bias.astype(f32)
    group_score = jnp.sum(lax.top_k(sel.reshape(-1, N_EXPERT_GROUPS, EXPERTS_PER_GROUP), TOP_K)[0], axis=-1)
    g_best = jnp.argmax(group_score, axis=-1)
    in_group = (jnp.arange(N_EXPERTS) // EXPERTS_PER_GROUP)[None, :] == g_best[:, None]
    _, idx = lax.top_k(jnp.where(in_group, sel, -jnp.inf), TOP_K)
    gate = jnp.take_along_axis(probs, idx, axis=-1)
    gate = gate / jnp.sum(gate, axis=-1, keepdims=True)
    gate_mat = jnp.sum(jax.nn.one_hot(idx, N_EXPERTS, dtype=f32) * gate[..., None], axis=1)
    out = jnp.zeros_like(t)
    for e in range(N_EXPERTS):
        h = jax.nn.silu(t @ wg[e]) * (t @ wu[e])
        out = out + gate_mat[:, e:e + 1] * (h @ wd[e])
    return out.reshape(bsz, seq, D_MODEL)


def _trunk(x, s5_re0, s5_im0, wkv0, shift0, params):
    (ln_gain, ln_bias, s5_lambda_re, s5_lambda_im, s5_log_step, s5_b_re, s5_b_im, s5_c_re, s5_c_im,
     s5_d, s5_w_out, s5_w_gate, rw_mix, rw_w_r, rw_w_k, rw_w_v, rw_w_o, rw_w0, rw_w1, rw_w2,
     rw_a0, rw_a1, rw_a2, rw_v0, rw_v1, rw_v2, rw_g1, rw_g2, rw_k_k, rw_k_a, rw_r_k, rw_lnx_g, rw_lnx_b,
     w_router, router_bias, moe_w_gate, moe_w_up, moe_w_down) = params
    new_re, new_im, new_wkv, new_shift = [], [], [], []
    v_first = None
    for i in range(DEPTH):
        j = i // N_MIXERS
        if i % N_MIXERS == 0:
            mixed, h_re, h_im = _s5_mixer(x, s5_re0[j], s5_im0[j], s5_lambda_re[j], s5_lambda_im[j],
                                          s5_log_step[j], s5_b_re[j], s5_b_im[j], s5_c_re[j], s5_c_im[j],
                                          s5_d[j], s5_w_out[j], s5_w_gate[j])
            new_re.append(h_re)
            new_im.append(h_im)
        else:
            vres = None if j == 0 else (rw_v0[j - 1], rw_v1[j - 1], rw_v2[j - 1])
            mixed, last_x, s_fin, v_first = _rwkv7_mixer(
                x, shift0[j], wkv0[j], v_first, vres, rw_mix[j], rw_w_r[j], rw_w_k[j], rw_w_v[j], rw_w_o[j],
                rw_w0[j], rw_w1[j], rw_w2[j], rw_a0[j], rw_a1[j], rw_a2[j], rw_g1[j], rw_g2[j],
                rw_k_k[j], rw_k_a[j], rw_r_k[j], rw_lnx_g[j], rw_lnx_b[j])
            new_wkv.append(s_fin)
            new_shift.append(last_x)
        x = _post_norm(x, mixed, ln_gain[i, 0], ln_bias[i, 0])
        x = _post_norm(x, _moe(x, w_router, router_bias, moe_w_gate[i], moe_w_up[i], moe_w_down[i]),
                       ln_gain[i, 1], ln_bias[i, 1])
    return x, jnp.stack(new_re), jnp.stack(new_im), jnp.stack(new_wkv), jnp.stack(new_shift)


def setup_inputs(seed: int = 0) -> dict:
    key = jax.random.key(seed)
    ks = iter(jax.random.split(key, 64))

    def nrm(shape, scale):
        return scale * jax.random.normal(next(ks), shape, f32)

    def uni(shape, lo, hi):
        return jax.random.uniform(next(ks), shape, f32, lo, hi)

    d_inv = D_MODEL ** -0.5
    n_idx = jnp.arange(S5_STATE, dtype=f32)
    return {
        'x_prompt': nrm((BATCH, SEQ, D_MODEL), 1.0),
        'x_sample': nrm((DEC_BATCH, DEC_SEQ, D_MODEL), 1.0),
        'state_s5_re': nrm((N_S5, DEC_BATCH, S5_GROUPS, S5_STATE), 0.1),
        'state_s5_im': nrm((N_S5, DEC_BATCH, S5_GROUPS, S5_STATE), 0.1),
        'state_wkv': nrm((N_RW, DEC_BATCH, RW_HEADS, HEAD_DIM, HEAD_DIM), 0.1),
        'state_shift': nrm((N_RW, DEC_BATCH, D_MODEL), 1.0),
        'ln_gain': 1.0 + nrm((DEPTH, 2, D_MODEL), 0.02),
        'ln_bias': nrm((DEPTH, 2, D_MODEL), 0.02),
        's5_lambda_re': -0.5 + nrm((N_S5, S5_GROUPS, S5_STATE), 0.01),
        's5_lambda_im': math.pi * n_idx + nrm((N_S5, S5_GROUPS, S5_STATE), 0.01),
        's5_log_step': uni((N_S5, S5_GROUPS), LOG_STEP_MIN, LOG_STEP_MAX),
        's5_b_re': nrm((N_S5, S5_GROUPS, S5_STATE, S5_GROUP), (2.0 * S5_GROUP) ** -0.5),
        's5_b_im': nrm((N_S5, S5_GROUPS, S5_STATE, S5_GROUP), (2.0 * S5_GROUP) ** -0.5),
        's5_c_re': nrm((N_S5, S5_GROUPS, S5_GROUP, S5_STATE), (2.0 * S5_STATE) ** -0.5),
        's5_c_im': nrm((N_S5, S5_GROUPS, S5_GROUP, S5_STATE), (2.0 * S5_STATE) ** -0.5),
        's5_d': nrm((N_S5, D_MODEL), 1.0),
        's5_w_out': nrm((N_S5, D_MODEL, D_MODEL), d_inv * BETA),
        's5_w_gate': nrm((N_S5, D_MODEL, D_MODEL), d_inv),
        'rw_mix': uni((N_RW, N_SHIFT_MIX, D_MODEL), 0.0, 1.0),
        'rw_w_r': nrm((N_RW, D_MODEL, D_MODEL), d_inv),
        'rw_w_k': nrm((N_RW, D_MODEL, D_MODEL), d_inv),
        'rw_w_v': nrm((N_RW, D_MODEL, D_MODEL), d_inv * BETA),
        'rw_w_o': nrm((N_RW, D_MODEL, D_MODEL), d_inv * BETA),
        'rw_w0': uni((N_RW, D_MODEL), -6.0, 1.0),
        'rw_w1': nrm((N_RW, D_MODEL, DECAY_LORA), 0.1 * d_inv),
        'rw_w2': nrm((N_RW, DECAY_LORA, D_MODEL), 0.1),
        'rw_a0': nrm((N_RW, D_MODEL), 0.1),
        'rw_a1': nrm((N_RW, D_MODEL, AAA_LORA), d_inv),
        'rw_a2': nrm((N_RW, AAA_LORA, D_MODEL), AAA_LORA ** -0.5),
        'rw_v0': nrm((N_RW - 1, D_MODEL), 0.1),
        'rw_v1': nrm((N_RW - 1, D_MODEL, MV_LORA), d_inv),
        'rw_v2': nrm((N_RW - 1, MV_LORA, D_MODEL), MV_LORA ** -0.5),
        'rw_g1': nrm((N_RW, D_MODEL, GATE_LORA), d_inv),
        'rw_g2': nrm((N_RW, GATE_LORA, D_MODEL), GATE_LORA ** -0.5),
        'rw_k_k': 0.85 + nrm((N_RW, D_MODEL), 0.02),
        'rw_k_a': 1.0 + nrm((N_RW, D_MODEL), 0.02),
        'rw_r_k': nrm((N_RW, RW_HEADS, HEAD_DIM), 0.1),
        'rw_lnx_g': 1.0 + nrm((N_RW, D_MODEL), 0.02),
        'rw_lnx_b': nrm((N_RW, D_MODEL), 0.02),
        'w_router': nrm((D_MODEL, N_EXPERTS), d_inv),
        'router_bias': nrm((N_EXPERTS,), 0.01),
        'moe_w_gate': nrm((DEPTH, N_EXPERTS, D_MODEL, D_FF_EXPERT), d_inv),
        'moe_w_up': nrm((DEPTH, N_EXPERTS, D_MODEL, D_FF_EXPERT), d_inv),
        'moe_w_down': nrm((DEPTH, N_EXPERTS, D_FF_EXPERT, D_MODEL), (D_FF_EXPERT ** -0.5) * BETA),
    }


def reference(x_prompt, x_sample, state_s5_re, state_s5_im, state_wkv, state_shift, ln_gain, ln_bias,
              s5_lambda_re, s5_lambda_im, s5_log_step, s5_b_re, s5_b_im, s5_c_re, s5_c_im, s5_d,
              s5_w_out, s5_w_gate, rw_mix, rw_w_r, rw_w_k, rw_w_v, rw_w_o, rw_w0, rw_w1, rw_w2,
              rw_a0, rw_a1, rw_a2, rw_v0, rw_v1, rw_v2, rw_g1, rw_g2, rw_k_k, rw_k_a, rw_r_k,
              rw_lnx_g, rw_lnx_b, w_router, router_bias, moe_w_gate, moe_w_up, moe_w_down):
    params = (ln_gain, ln_bias, s5_lambda_re, s5_lambda_im, s5_log_step, s5_b_re, s5_b_im, s5_c_re, s5_c_im,
              s5_d, s5_w_out, s5_w_gate, rw_mix, rw_w_r, rw_w_k, rw_w_v, rw_w_o, rw_w0, rw_w1, rw_w2,
              rw_a0, rw_a1, rw_a2, rw_v0, rw_v1, rw_v2, rw_g1, rw_g2, rw_k_k, rw_k_a, rw_r_k, rw_lnx_g, rw_lnx_b,
              w_router, router_bias, moe_w_gate, moe_w_up, moe_w_down)
    bsz_p = x_prompt.shape[0]
    zero_s5 = jnp.zeros((N_S5, bsz_p, S5_GROUPS, S5_STATE), f32)
    zero_wkv = jnp.zeros((N_RW, bsz_p, RW_HEADS, HEAD_DIM, HEAD_DIM), f32)
    zero_shift = jnp.zeros((N_RW, bsz_p, D_MODEL), f32)
    y_prompt, p_re, p_im, p_wkv, p_shift = _trunk(x_prompt, zero_s5, zero_s5, zero_wkv, zero_shift, params)
    y_sample, s_re, s_im, s_wkv, s_shift = _trunk(x_sample, state_s5_re, state_s5_im, state_wkv, state_shift,
                                                  params)
    return (y_prompt, y_sample, p_re, p_im, p_wkv, p_shift, s_re, s_im, s_wkv, s_shift)
```

```python
import functools
import math

import jax
import jax.numpy as jnp
from jax import lax
from jax.experimental import pallas as pl
from jax.experimental.pallas import tpu as pltpu

f32 = jnp.float32
bf16 = jnp.bfloat16
i32 = jnp.int32

S5_GROUP = 16
S5_STATE = 64
HEAD_DIM = 64
GN_EPS = 64e-5
LN_EPS = 1e-5
N_EXPERTS = 16
N_EXPERT_GROUPS = 4
EXPERTS_PER_GROUP = N_EXPERTS // N_EXPERT_GROUPS

SUBLANES = 8
LANES = 128
MXU_DIM = 256
VMEM_LIMIT = 56 * 1024 * 1024

TOK_TILE = 512
PROJ_TILE = 256
S5_ROWS = 256
EXPERT_TILE = 256
COMBINE_TILE = 256
DISPATCH_TILE = 512
WKV_STEPS = 16


def _cparams(n_axes=1, **kw):
    return pltpu.CompilerParams(dimension_semantics=("arbitrary",) * n_axes,
                                vmem_limit_bytes=VMEM_LIMIT, **kw)


def _bdot(a, w):
    return jnp.dot(a.astype(bf16), w, preferred_element_type=f32)


def _layer_norm(y, g, b):
    mu = jnp.mean(y, axis=-1, keepdims=True)
    d = y - mu
    var = jnp.mean(d * d, axis=-1, keepdims=True)
    return d * lax.rsqrt(var + LN_EPS) * g + b


def _aligned(idx):
    return idx if isinstance(idx, int) else pl.multiple_of(idx, SUBLANES)


def _softplus(x):
    return jnp.maximum(x, 0.0) + jnp.log1p(jnp.exp(-jnp.abs(x)))


def _route(x1, wr_hi, wr_lo, rbias, tri, cnt_ref, ri_ref, rf_ref, cnt_out_ref):
    tm = x1.shape[0]
    xh = x1.astype(bf16)
    xl = (x1 - xh.astype(f32)).astype(bf16)
    dn = (((1,), (1,)), ((), ()))
    logits = (lax.dot_general(wr_hi, xh, dn, preferred_element_type=f32)
              + lax.dot_general(wr_lo, xh, dn, preferred_element_type=f32)
              + lax.dot_general(wr_hi, xl, dn, preferred_element_type=f32))
    m = jnp.max(logits, axis=0, keepdims=True)
    ex = jnp.exp(logits - m)
    probs = ex / jnp.sum(ex, axis=0, keepdims=True)
    sel = probs + rbias
    s = [sel[e:e + 1, :] for e in range(N_EXPERTS)]
    p = [probs[e:e + 1, :] for e in range(N_EXPERTS)]

    best = None
    gi = None
    for g in range(N_EXPERT_GROUPS):
        a, b, c, d = s[4 * g:4 * g + 4]
        hi1, lo1 = jnp.maximum(a, b), jnp.minimum(a, b)
        hi2, lo2 = jnp.maximum(c, d), jnp.minimum(c, d)
        score = jnp.maximum(hi1, hi2) + jnp.maximum(jnp.minimum(hi1, hi2), jnp.maximum(lo1, lo2))
        if g == 0:
            best, gi = score, jnp.zeros(score.shape, i32)
        else:
            better = score > best
            best = jnp.where(better, score, best)
            gi = jnp.where(better, g, gi)

    def in_group(rows, j):
        out = rows[12 + j]
        for g in (2, 1, 0):
            out = jnp.where(gi == g, rows[4 * g + j], out)
        return out

    v = [in_group(s, j) for j in range(EXPERTS_PER_GROUP)]
    pv = [in_group(p, j) for j in range(EXPERTS_PER_GROUP)]
    order = []
    for j in range(EXPERTS_PER_GROUP):
        r = jnp.zeros(v[j].shape, i32)
        for i in range(EXPERTS_PER_GROUP):
            if i < j:
                r = r + jnp.where(v[i] >= v[j], 1, 0)
            elif i > j:
                r = r + jnp.where(v[i] > v[j], 1, 0)
        order.append(r)
    j1 = sum(jnp.where(order[j] == 0, j, 0) for j in range(EXPERTS_PER_GROUP))
    j2 = sum(jnp.where(order[j] == 1, j, 0) for j in range(EXPERTS_PER_GROUP))
    p1 = sum(jnp.where(order[j] == 0, pv[j], 0.0) for j in range(EXPERTS_PER_GROUP))
    p2 = sum(jnp.where(order[j] == 1, pv[j], 0.0) for j in range(EXPERTS_PER_GROUP))
    e1 = gi * EXPERTS_PER_GROUP + j1
    e2 = gi * EXPERTS_PER_GROUP + j2
    den = p1 + p2
    g1 = p1 / den
    g2 = p2 / den

    eio = lax.broadcasted_iota(i32, (N_EXPERTS, tm), 0)
    oh1 = eio == e1
    oh2 = eio == e2
    oh = jnp.where(oh1, 1.0, 0.0) + jnp.where(oh2, 1.0, 0.0)
    before = jnp.dot(oh.astype(bf16), tri, preferred_element_type=f32)
    before = before + cnt_ref[:, 0:1]
    r1 = jnp.sum(jnp.where(oh1, before, 0.0), axis=0, keepdims=True)
    r2 = jnp.sum(jnp.where(oh2, before, 0.0), axis=0, keepdims=True)
    cnt_new = cnt_ref[...] + jnp.sum(oh, axis=1, keepdims=True)
    cnt_ref[...] = cnt_new
    cnt_out_ref[...] = cnt_new

    ri_ref[...] = jnp.zeros(ri_ref.shape, i32)
    rf_ref[...] = jnp.zeros(rf_ref.shape, f32)
    ri_ref[0:1, :] = e1
    ri_ref[1:2, :] = e2
    ri_ref[2:3, :] = r1.astype(i32)
    ri_ref[3:4, :] = r2.astype(i32)
    rf_ref[0:1, :] = g1
    rf_ref[1:2, :] = g2


def _s5_disc_kernel(lre_ref, lim_ref, lstep_ref, bre_ref, bim_ref,
                    are_ref, aim_ref, bbre_ref, bbim_ref):
    lre = lre_ref[...]
    lim = lim_ref[...]
    step = jnp.exp(lstep_ref[...])
    mag = jnp.exp(lre * step)
    th = lim * step
    a_re = mag * jnp.cos(th)
    a_im = mag * jnp.sin(th)
    den = lre * lre + lim * lim
    q_re = ((a_re - 1.0) * lre + a_im * lim) / den
    q_im = (a_im * lre - (a_re - 1.0) * lim) / den
    are_ref[...] = a_re
    aim_ref[...] = a_im
    b_re = bre_ref[...]
    b_im = bim_ref[...]
    bbre_ref[...] = q_re[None] * b_re - q_im[None] * b_im
    bbim_ref[...] = q_re[None] * b_im + q_im[None] * b_re


def _s5_discretise(lam_re, lam_im, log_step, b_re, b_im):
    g, p = lam_re.shape
    c = b_re.shape[-1]
    outs = pl.pallas_call(
        _s5_disc_kernel,
        out_shape=(jax.ShapeDtypeStruct((g, p), f32), jax.ShapeDtypeStruct((g, p), f32),
                   jax.ShapeDtypeStruct((c, g, p), f32), jax.ShapeDtypeStruct((c, g, p), f32)),
        name="s5_discretise",
    )(lam_re, lam_im, log_step.reshape(g, 1),
      jnp.transpose(b_re, (2, 0, 1)), jnp.transpose(b_im, (2, 0, 1)))
    return outs


def _block_diag(w, n_blk):
    s, _, a, b = w.shape
    eye = jnp.eye(n_blk, dtype=w.dtype)
    out = w[:, :, :, None, :] * eye[None, :, None, :, None]
    return out.reshape(s, n_blk * a, n_blk * b)


def _s5_kernel(u_ref, h0re_ref, h0im_ref, are_ref, aim_ref, wb_ref, wcre_ref, wcim_ref, d_ref,
               z_ref, hre_ref, him_ref, bu_ref, *, bn, tb):
    i = pl.program_id(0)
    half = are_ref.shape[1]
    n_slab = wb_ref.shape[0]
    kw = wb_ref.shape[1]
    sw = half // n_slab
    cw = 1024

    @pl.when(i == 0)
    def _():
        hre_ref[...] = h0re_ref[...]
        him_ref[...] = h0im_ref[...]

    ub = u_ref[...].astype(bf16)
    for s in range(n_slab):
        res = jnp.dot(ub[:, s * kw:(s + 1) * kw], wb_ref[s], preferred_element_type=f32)
        bu_ref[:, s * sw:(s + 1) * sw] = res[:, :sw]
        bu_ref[:, half + s * sw:half + (s + 1) * sw] = res[:, sw:]

    def advance(t, first):
        def sub(j, carry):
            rr = _aligned(t * bn + j * SUBLANES)
            for c0 in range(0, half, cw):
                if first:
                    jr = _aligned(j * SUBLANES)
                    p_re = hre_ref[pl.ds(jr, SUBLANES), c0:c0 + cw]
                    p_im = him_ref[pl.ds(jr, SUBLANES), c0:c0 + cw]
                else:
                    pr = _aligned(rr - bn)
                    p_re = bu_ref[pl.ds(pr, SUBLANES), c0:c0 + cw]
                    p_im = bu_ref[pl.ds(pr, SUBLANES), half + c0:half + c0 + cw]
                a_re = are_ref[:, c0:c0 + cw]
                a_im = aim_ref[:, c0:c0 + cw]
                n_re = a_re * p_re - a_im * p_im + bu_ref[pl.ds(rr, SUBLANES), c0:c0 + cw]
                n_im = a_re * p_im + a_im * p_re + bu_ref[pl.ds(rr, SUBLANES), half + c0:half + c0 + cw]
                bu_ref[pl.ds(rr, SUBLANES), c0:c0 + cw] = n_re
                bu_ref[pl.ds(rr, SUBLANES), half + c0:half + c0 + cw] = n_im
            return carry
        if bn == SUBLANES:
            sub(0, 0)
        else:
            lax.fori_loop(0, bn // SUBLANES, sub, 0)

    advance(0, True)
    if tb > 1:
        def body(t, carry):
            advance(t, False)
            return carry
        lax.fori_loop(1, tb, body, 0)
    hre_ref[...] = bu_ref[(tb - 1) * bn:tb * bn, 0:half]
    him_ref[...] = bu_ref[(tb - 1) * bn:tb * bn, half:2 * half]

    nw = wcre_ref.shape[2]
    for n in range(wcre_ref.shape[0]):
        h_re = bu_ref[:, n * sw:(n + 1) * sw]
        h_im = bu_ref[:, half + n * sw:half + (n + 1) * sw]
        y = _bdot(h_re, wcre_ref[n]) - _bdot(h_im, wcim_ref[n])
        y = y + d_ref[:, n * nw:(n + 1) * nw] * u_ref[:, n * nw:(n + 1) * nw]
        z_ref[:, n * nw:(n + 1) * nw] = jax.nn.gelu(y, approximate=True).astype(z_ref.dtype)


def _s5_scan(x, row0, n_steps, bn, h0_re, h0_im, a_re8, a_im8, wb, wc_re, wc_im, d_skip):
    d = x.shape[1]
    tb = max(1, S5_ROWS // bn)
    rows = tb * bn
    assert n_steps % tb == 0 and row0 % rows == 0
    half = a_re8.shape[1]
    blk0 = row0 // rows
    const2 = lambda i: (0, 0)
    const3 = lambda i: (0, 0, 0)
    z, h_re, h_im = pl.pallas_call(
        functools.partial(_s5_kernel, bn=bn, tb=tb),
        grid=(n_steps // tb,),
        in_specs=[
            pl.BlockSpec((rows, d), lambda i: (i + blk0, 0)),
            pl.BlockSpec((bn, half), const2),
            pl.BlockSpec((bn, half), const2),
            pl.BlockSpec(a_re8.shape, const2),
            pl.BlockSpec(a_im8.shape, const2),
            pl.BlockSpec(wb.shape, const3),
            pl.BlockSpec(wc_re.shape, const3),
            pl.BlockSpec(wc_im.shape, const3),
            pl.BlockSpec((1, d), const2),
        ],
        out_specs=[
            pl.BlockSpec((rows, d), lambda i: (i, 0)),
            pl.BlockSpec((bn, half), const2),
            pl.BlockSpec((bn, half), const2),
        ],
        out_shape=(jax.ShapeDtypeStruct((n_steps * bn, d), bf16),
                   jax.ShapeDtypeStruct((bn, half), f32),
                   jax.ShapeDtypeStruct((bn, half), f32)),
        scratch_shapes=[pltpu.VMEM((rows, 2 * half), f32)],
        compiler_params=_cparams(1),
        name="s5_scan",
    )(x, h0_re, h0_im, a_re8, a_im8, wb, wc_re, wc_im, d_skip)
    return z, h_re, h_im


def _s5_out_kernel(z_ref, x_ref, wo_ref, wg_ref, lng_ref, lnb_ref,
                   wrh_ref, wrl_ref, rb_ref, tri_ref,
                   x1_ref, ri_ref, rf_ref, cnt_out_ref, cnt_ref, *, alpha):
    @pl.when(pl.program_id(0) == 0)
    def _():
        cnt_ref[...] = jnp.zeros(cnt_ref.shape, f32)

    z = z_ref[...]
    mixed = (jnp.dot(z, wo_ref[...], preferred_element_type=f32)
             * jax.nn.sigmoid(jnp.dot(z, wg_ref[...], preferred_element_type=f32)))
    x1 = _layer_norm(alpha * x_ref[...] + mixed, lng_ref[...], lnb_ref[...])
    x1_ref[...] = x1
    _route(x1, wrh_ref[...], wrl_ref[...], rb_ref[...], tri_ref[...], cnt_ref, ri_ref, rf_ref,
           cnt_out_ref)


def _rw_out_kernel(o_ref, g_ref, x_ref, wo_ref, lng_ref, lnb_ref,
                   wrh_ref, wrl_ref, rb_ref, tri_ref,
                   x1_ref, ri_ref, rf_ref, cnt_out_ref, cnt_ref, *, alpha):
    @pl.when(pl.program_id(0) == 0)
    def _():
        cnt_ref[...] = jnp.zeros(cnt_ref.shape, f32)

    mixed = _bdot(o_ref[...] * g_ref[...], wo_ref[...])
    x1 = _layer_norm(alpha * x_ref[...] + mixed, lng_ref[...], lnb_ref[...])
    x1_ref[...] = x1
    _route(x1, wrh_ref[...], wrl_ref[...], rb_ref[...], tri_ref[...], cnt_ref, ri_ref, rf_ref,
           cnt_out_ref)


def _post_mixer(kernel_fn, name, tok_inputs, weights, lng, lnb, router, alpha):
    t, d = tok_inputs[-1].shape
    tm = TOK_TILE
    assert t % tm == 0
    wr_hi, wr_lo, rbias, tri = router
    tile = lambda i: (i, 0)
    const = lambda i: (0, 0)
    in_specs = ([pl.BlockSpec((tm, d), tile) for _ in tok_inputs]
                + [pl.BlockSpec(w.shape, const) for w in weights]
                + [pl.BlockSpec((1, d), const), pl.BlockSpec((1, d), const),
                   pl.BlockSpec(wr_hi.shape, const), pl.BlockSpec(wr_lo.shape, const),
                   pl.BlockSpec(rbias.shape, const), pl.BlockSpec(tri.shape, const)])
    x1, ri, rf, cnt = pl.pallas_call(
        functools.partial(kernel_fn, alpha=alpha),
        grid=(t // tm,),
        in_specs=in_specs,
        out_specs=[pl.BlockSpec((tm, d), tile),
                   pl.BlockSpec((SUBLANES, tm), lambda i: (0, i)),
                   pl.BlockSpec((SUBLANES, tm), lambda i: (0, i)),
                   pl.BlockSpec((N_EXPERTS, LANES), const)],
        out_shape=(jax.ShapeDtypeStruct((t, d), f32),
                   jax.ShapeDtypeStruct((SUBLANES, t), i32),
                   jax.ShapeDtypeStruct((SUBLANES, t), f32),
                   jax.ShapeDtypeStruct((N_EXPERTS, LANES), f32)),
        scratch_shapes=[pltpu.VMEM((N_EXPERTS, LANES), f32)],
        compiler_params=_cparams(1),
        name=name,
    )(*tok_inputs, *weights, lng, lnb, wr_hi, wr_lo, rbias, tri)
    return x1, ri, rf, cnt


def _dispatch_kernel(d1_ref, d2_ref, x_hbm, xs_in_hbm, xs_hbm, sem, *, tile):
    del xs_in_hbm
    base = pl.program_id(0) * tile

    def row_copy(row, dst):
        return pltpu.make_async_copy(x_hbm.at[pl.ds(row, 1)], xs_hbm.at[pl.ds(dst, 1)], sem)

    def start(j, carry):
        row = base + j
        row_copy(row, d1_ref[row]).start()
        row_copy(row, d2_ref[row]).start()
        return carry
    lax.fori_loop(0, tile, start, 0)

    def wait(j, carry):
        row_copy(0, 0).wait()
        row_copy(0, 0).wait()
        return carry
    lax.fori_loop(0, tile, wait, 0)


def _dispatch(x1, d1, d2, n_rows):
    t, d = x1.shape
    tile = DISPATCH_TILE
    assert t % tile == 0
    xs0 = jnp.zeros((n_rows, d), x1.dtype)
    return pl.pallas_call(
        functools.partial(_dispatch_kernel, tile=tile),
        grid_spec=pltpu.PrefetchScalarGridSpec(
            num_scalar_prefetch=2,
            grid=(t // tile,),
            in_specs=[pl.BlockSpec(memory_space=pl.ANY), pl.BlockSpec(memory_space=pl.ANY)],
            out_specs=pl.BlockSpec(memory_space=pl.ANY),
            scratch_shapes=[pltpu.SemaphoreType.DMA(())],
        ),
        out_shape=jax.ShapeDtypeStruct((n_rows, d), x1.dtype),
        input_output_aliases={3: 0},
        compiler_params=_cparams(1, has_side_effects=True),
        name="moe_dispatch",
    )(d1, d2, x1, xs0)


def _expert_kernel(te_ref, nu_ref, xs_ref, wg_ref, wu_ref, wd_ref, ys_ref, wgb, wub, wdb):
    i = pl.program_id(0)
    used = i < nu_ref[0]
    prev = te_ref[jnp.maximum(i - 1, 0)]
    fresh = jnp.logical_or(i == 0, te_ref[i] != prev)

    @pl.when(jnp.logical_and(used, fresh))
    def _():
        wgb[...] = wg_ref[0].astype(bf16)
        wub[...] = wu_ref[0].astype(bf16)
        wdb[...] = wd_ref[0].astype(bf16)

    @pl.when(used)
    def _():
        x = xs_ref[...].astype(bf16)
        h = (jax.nn.silu(jnp.dot(x, wgb[...], preferred_element_type=f32))
             * jnp.dot(x, wub[...], preferred_element_type=f32))
        ys_ref[...] = jnp.dot(h.astype(bf16), wdb[...], preferred_element_type=f32)

    @pl.when(jnp.logical_not(used))
    def _():
        ys_ref[...] = jnp.zeros(ys_ref.shape, ys_ref.dtype)


def _experts(xs, tile_expert, n_used, wg, wu, wd):
    n_rows, d = xs.shape
    tile = EXPERT_TILE
    n_tiles = n_rows // tile
    dff = wg.shape[2]

    def row_map(i, te, nu):
        return (jnp.minimum(i, nu[0] - 1), 0)

    def w_map(i, te, nu):
        return (te[i], 0, 0)

    return pl.pallas_call(
        _expert_kernel,
        grid_spec=pltpu.PrefetchScalarGridSpec(
            num_scalar_prefetch=2,
            grid=(n_tiles,),
            in_specs=[pl.BlockSpec((tile, d), row_map),
                      pl.BlockSpec((1, d, dff), w_map),
                      pl.BlockSpec((1, d, dff), w_map),
                      pl.BlockSpec((1, dff, d), w_map)],
            out_specs=pl.BlockSpec((tile, d), lambda i, te, nu: (i, 0)),
            scratch_shapes=[pltpu.VMEM((d, dff), bf16), pltpu.VMEM((d, dff), bf16),
                            pltpu.VMEM((dff, d), bf16)],
        ),
        out_shape=jax.ShapeDtypeStruct((n_rows, d), f32),
        compiler_params=_cparams(1),
        name="moe_experts",
    )(tile_expert, n_used, xs, wg, wu, wd)


def _combine_kernel(d1_ref, d2_ref, ys_hbm, x1_ref, gate_ref, lng_ref, lnb_ref, out_ref,
                    y1_buf, y2_buf, sem, *, tile, alpha):
    base = pl.program_id(0) * tile

    def row_copy(src, buf, j):
        return pltpu.make_async_copy(ys_hbm.at[pl.ds(src, 1)], buf.at[pl.ds(j, 1)], sem)

    def start(j, carry):
        row_copy(d1_ref[base + j], y1_buf, j).start()
        row_copy(d2_ref[base + j], y2_buf, j).start()
        return carry
    lax.fori_loop(0, tile, start, 0)

    def wait(j, carry):
        row_copy(0, y1_buf, 0).wait()
        row_copy(0, y2_buf, 0).wait()
        return carry
    lax.fori_loop(0, tile, wait, 0)

    gates = gate_ref[...]
    moe = gates[:, 0:1] * y1_buf[...] + gates[:, 1:2] * y2_buf[...]
    out_ref[...] = _layer_norm(alpha * x1_ref[...] + moe, lng_ref[...], lnb_ref[...])


def _combine(ys, x1, d1, d2, gates, lng, lnb, alpha):
    t, d = x1.shape
    tile = COMBINE_TILE
    assert t % tile == 0
    tok = lambda i, a, b: (i, 0)
    const = lambda i, a, b: (0, 0)
    return pl.pallas_call(
        functools.partial(_combine_kernel, tile=tile, alpha=alpha),
        grid_spec=pltpu.PrefetchScalarGridSpec(
            num_scalar_prefetch=2,
            grid=(t // tile,),
            in_specs=[pl.BlockSpec(memory_space=pl.ANY),
                      pl.BlockSpec((tile, d), tok),
                      pl.BlockSpec((tile, 2), tok),
                      pl.BlockSpec((1, d), const),
                      pl.BlockSpec((1, d), const)],
            out_specs=pl.BlockSpec((tile, d), tok),
            scratch_shapes=[pltpu.VMEM((tile, d), f32), pltpu.VMEM((tile, d), f32),
                            pltpu.SemaphoreType.DMA(())],
        ),
        out_shape=jax.ShapeDtypeStruct((t, d), f32),
        compiler_params=_cparams(1),
        name="moe_combine",
    )(d1, d2, ys, x1, gates, lng, lnb)


def _moe(x1, ri, rf, cnt, wg, wu, wd, lng, lnb, alpha):
    t, d = x1.shape
    tile = EXPERT_TILE
    n_tiles = -(-2 * t // tile) + N_EXPERTS
    counts = cnt[:, 0].astype(i32)
    padded = ((counts + tile - 1) // tile) * tile
    ends = jnp.cumsum(padded)
    starts = ends - padded
    n_used = (ends[-1] // tile).astype(i32).reshape(1)
    d1 = starts[ri[0]] + ri[2]
    d2 = starts[ri[1]] + ri[3]
    tile_start = jnp.minimum(jnp.arange(n_tiles, dtype=i32), n_used[0] - 1) * tile
    tile_expert = jnp.sum(tile_start[:, None] >= ends[None, :], axis=1).astype(i32)
    xs = _dispatch(x1, d1, d2, n_tiles * tile)
    ys = _experts(xs, tile_expert, n_used, wg, wu, wd)
    gates = jnp.transpose(rf[0:2])
    return _combine(ys, x1, d1, d2, gates, lng, lnb, alpha)


def _rw_proj_kernel(*refs, has_vres):
    if has_vres:
        (x_ref, xp_ref, vf_ref, mix_ref, wr_ref, wk_ref, wv_ref, w0_ref, w1_ref, w2_ref,
         a0_ref, a1_ref, a2_ref, g1_ref, g2_ref, kk_ref, ka_ref, v0_ref, v1_ref, v2_ref,
         r_out, w_out, k_out, v_out, kk_out, a_out, g_out) = refs
    else:
        (x_ref, xp_ref, mix_ref, wr_ref, wk_ref, wv_ref, w0_ref, w1_ref, w2_ref,
         a0_ref, a1_ref, a2_ref, g1_ref, g2_ref, kk_ref, ka_ref,
         r_out, w_out, k_out, v_out, kk_out, a_out, g_out) = refs
    x = x_ref[...]
    xx = xp_ref[...] - x

    def mixed(i):
        return (x + xx * mix_ref[i:i + 1, :]).astype(bf16)

    r_out[...] = jnp.dot(mixed(0), wr_ref[...], preferred_element_type=f32)
    lora_w = jnp.tanh(jnp.dot(mixed(1), w1_ref[...], preferred_element_type=f32))
    w = -_softplus(-(w0_ref[...] + _bdot(lora_w, w2_ref[...]))) - 0.5
    w_out[...] = jnp.exp(-jnp.exp(w))
    k = jnp.dot(mixed(2), wk_ref[...], preferred_element_type=f32)
    xv = mixed(3)
    v = jnp.dot(xv, wv_ref[...], preferred_element_type=f32)
    a = jax.nn.sigmoid(a0_ref[...] + _bdot(jnp.dot(mixed(4), a1_ref[...], preferred_element_type=f32),
                                           a2_ref[...]))
    a_out[...] = a
    gate = jax.nn.sigmoid(jnp.dot(mixed(5), g1_ref[...], preferred_element_type=f32))
    g_out[...] = _bdot(gate, g2_ref[...])
    kk_out[...] = k * kk_ref[...]
    k_out[...] = k * (1.0 + (a - 1.0) * ka_ref[...])
    if has_vres:
        lora_v = _bdot(jnp.dot(xv, v1_ref[...], preferred_element_type=f32), v2_ref[...])
        v = v + (vf_ref[...] - v) * jax.nn.sigmoid(v0_ref[...] + lora_v)
    v_out[...] = v


def _rw_proj(x, x_prev, v_first, weights):
    t, d = x.shape
    tm = PROJ_TILE
    assert t % tm == 0
    has_vres = v_first is not None
    tok = [x, x_prev] + ([v_first] if has_vres else [])
    tile = lambda i: (i, 0)
    const = lambda i: (0, 0)
    outs = pl.pallas_call(
        functools.partial(_rw_proj_kernel, has_vres=has_vres),
        grid=(t // tm,),
        in_specs=([pl.BlockSpec((tm, d), tile) for _ in tok]
                  + [pl.BlockSpec(w.shape, const) for w in weights]),
        out_specs=[pl.BlockSpec((tm, d), tile) for _ in range(7)],
        out_shape=tuple(jax.ShapeDtypeStruct((t, d), f32) for _ in range(7)),
        compiler_params=_cparams(1),
        name="rwkv_proj",
    )(*tok, *weights)
    return outs


def _wkv_kernel(r_ref, w_ref, k_ref, v_ref, kk_ref, a_ref, s0_ref, lng_ref, lnb_ref, rk_ref,
                o_ref, s_ref, nkk_buf, b_buf, *, steps):
    n = s_ref.shape[0]
    ti = pl.program_id(1)

    @pl.when(ti == 0)
    def _():
        s_ref[...] = s0_ref[...]

    def bcast(ref, idx):
        row = ref[pl.ds(idx, 1), :]
        return jnp.broadcast_to(row, (n, row.shape[1]))

    def one_step(t, carry):
        kk_raw = kk_ref[t]
        norm = jnp.sqrt(jnp.sum(kk_raw * kk_raw, axis=0, keepdims=True))
        kk = kk_raw / jnp.maximum(norm, 1e-12)
        nkk_buf[...] = -kk
        b_buf[...] = kk * a_ref[t]
        vv = v_ref[t]
        r_t = r_ref[t]
        k_t = k_ref[t]

        def pass1(kq, sa):
            for u in range(SUBLANES):
                kidx = kq * SUBLANES + u
                sa = sa + s_ref[kidx] * bcast(nkk_buf, kidx)
            return sa
        sa = lax.fori_loop(0, n // SUBLANES, pass1, jnp.zeros(vv.shape, f32))

        r_row = r_ref.at[t]
        w_row = w_ref.at[t]
        k_row = k_ref.at[t]

        def pass2(kq, out):
            for u in range(SUBLANES):
                kidx = kq * SUBLANES + u
                s_new = (s_ref[kidx] * bcast(w_row, kidx) + sa * bcast(b_buf, kidx)
                         + vv * bcast(k_row, kidx))
                s_ref[kidx] = s_new
                out = out + s_new * bcast(r_row, kidx)
            return out
        out = lax.fori_loop(0, n // SUBLANES, pass2, jnp.zeros(vv.shape, f32))

        mu = jnp.mean(out, axis=0, keepdims=True)
        dlt = out - mu
        var = jnp.mean(dlt * dlt, axis=0, keepdims=True)
        o_n = dlt * lax.rsqrt(var + GN_EPS) * lng_ref[...] + lnb_ref[...]
        bonus = jnp.sum(r_t * k_t * rk_ref[...], axis=0, keepdims=True) * vv
        o_ref[t] = o_n + bonus
        return carry

    lax.fori_loop(0, steps, one_step, 0)


def _wkv_scan(r, w, k, v, kk, a, s0, lng, lnb, rk):
    n_steps, n, p = r.shape
    steps = min(WKV_STEPS, n_steps)
    assert n_steps % steps == 0 and p % LANES == 0
    seq = lambda j, t: (t, 0, j)
    per_lane = lambda j, t: (0, 0, j)
    par = lambda j, t: (0, j)
    o, s_fin = pl.pallas_call(
        functools.partial(_wkv_kernel, steps=steps),
        grid=(p // LANES, n_steps // steps),
        in_specs=([pl.BlockSpec((steps, n, LANES), seq) for _ in range(6)]
                  + [pl.BlockSpec((n, n, LANES), per_lane)]
                  + [pl.BlockSpec((n, LANES), par) for _ in range(3)]),
        out_specs=[pl.BlockSpec((steps, n, LANES), seq),
                   pl.BlockSpec((n, n, LANES), per_lane)],
        out_shape=(jax.ShapeDtypeStruct((n_steps, n, p), f32),
                   jax.ShapeDtypeStruct((n, n, p), f32)),
        scratch_shapes=[pltpu.VMEM((n, LANES), f32), pltpu.VMEM((n, LANES), f32)],
        compiler_params=_cparams(2),
        name="wkv_scan",
    )(r, w, k, v, kk, a, s0, lng, lnb, rk)
    return o, s_fin


def _to_scan(x_tok, n_steps, bn):
    h = x_tok.shape[1] // HEAD_DIM
    y = x_tok.reshape(n_steps, bn, h, HEAD_DIM)
    return jnp.transpose(y, (0, 3, 1, 2)).reshape(n_steps, HEAD_DIM, bn * h)


def _from_scan(y, n_steps, bn):
    h = y.shape[2] // bn
    y = y.reshape(n_steps, HEAD_DIM, bn, h)
    return jnp.transpose(y, (0, 2, 3, 1)).reshape(n_steps * bn, h * HEAD_DIM)


def _lanes_param(p, bn):
    h = p.size // HEAD_DIM
    y = jnp.transpose(p.reshape(h, HEAD_DIM))
    return jnp.tile(y, (1, bn))


def _pad_cols(w, mult=LANES):
    pad = (-w.shape[1]) % mult
    return jnp.pad(w, ((0, 0), (0, pad)))


def _pad_rows(w, mult=LANES):
    pad = (-w.shape[0]) % mult
    return jnp.pad(w, ((0, pad), (0, 0)))


def kernel(x_prompt, x_sample, state_s5_re, state_s5_im, state_wkv, state_shift, ln_gain, ln_bias,
           s5_lambda_re, s5_lambda_im, s5_log_step, s5_b_re, s5_b_im, s5_c_re, s5_c_im, s5_d,
           s5_w_out, s5_w_gate, rw_mix, rw_w_r, rw_w_k, rw_w_v, rw_w_o, rw_w0, rw_w1, rw_w2,
           rw_a0, rw_a1, rw_a2, rw_v0, rw_v1, rw_v2, rw_g1, rw_g2, rw_k_k, rw_k_a, rw_r_k,
           rw_lnx_g, rw_lnx_b, w_router, router_bias, moe_w_gate, moe_w_up, moe_w_down):
    bp, lp, d = x_prompt.shape
    bs, ls, _ = x_sample.shape
    tp, ts = bp * lp, bs * ls
    depth = ln_gain.shape[0]
    alpha = (2.0 * depth) ** 0.25
    n_groups = s5_lambda_re.shape[1]
    n_state = s5_lambda_re.shape[2]
    half = n_groups * n_state
    heads = d // HEAD_DIM

    x = jnp.concatenate([jnp.transpose(x_prompt, (1, 0, 2)).reshape(tp, d),
                         jnp.transpose(x_sample, (1, 0, 2)).reshape(ts, d)], axis=0)

    wr_t = jnp.transpose(w_router)
    wr_hi = wr_t.astype(bf16)
    wr_lo = (wr_t - wr_hi.astype(f32)).astype(bf16)
    rbias = router_bias.reshape(N_EXPERTS, 1)
    tri = jnp.triu(jnp.ones((TOK_TILE, TOK_TILE), bf16), k=1)
    router = (wr_hi, wr_lo, rbias, tri)

    new_re_p, new_im_p, new_re_s, new_im_s = [], [], [], []
    new_wkv_p, new_wkv_s, new_shift_p, new_shift_s = [], [], [], []
    v_first = None
    groups_per_slab = MXU_DIM // S5_GROUP

    for i in range(depth):
        j = i // 2
        if i % 2 == 0:
            a_re, a_im, bb_re, bb_im = _s5_discretise(s5_lambda_re[j], s5_lambda_im[j], s5_log_step[j],
                                                      s5_b_re[j], s5_b_im[j])
            a_re8 = jnp.broadcast_to(a_re.reshape(1, half), (SUBLANES, half))
            a_im8 = jnp.broadcast_to(a_im.reshape(1, half), (SUBLANES, half))
            n_slab = n_groups // groups_per_slab

            def pack_b(bb):
                w = jnp.transpose(bb, (1, 0, 2)).reshape(n_slab, groups_per_slab, S5_GROUP, n_state)
                return _block_diag(w, groups_per_slab)
            wb = jnp.concatenate([pack_b(bb_re), pack_b(bb_im)], axis=2).astype(bf16)

            def pack_c(c):
                w = jnp.transpose(c, (0, 2, 1)).reshape(n_slab, groups_per_slab, n_state, S5_GROUP)
                return _block_diag(w, groups_per_slab).astype(bf16)
            wc_re, wc_im = pack_c(s5_c_re[j]), pack_c(s5_c_im[j])
            d_skip = s5_d[j].reshape(1, d)

            zero_h = jnp.zeros((bp, half), f32)
            z_p, hre_p, him_p = _s5_scan(x, 0, lp, bp, zero_h, zero_h, a_re8, a_im8,
                                         wb, wc_re, wc_im, d_skip)
            z_s, hre_s, him_s = _s5_scan(x, tp, ls, bs, state_s5_re[j].reshape(bs, half),
                                         state_s5_im[j].reshape(bs, half), a_re8, a_im8,
                                         wb, wc_re, wc_im, d_skip)
            new_re_p.append(hre_p.reshape(bp, n_groups, n_state))
            new_im_p.append(him_p.reshape(bp, n_groups, n_state))
            new_re_s.append(hre_s.reshape(bs, n_groups, n_state))
            new_im_s.append(him_s.reshape(bs, n_groups, n_state))
            z = jnp.concatenate([z_p, z_s], axis=0)
            x1, ri, rf, cnt = _post_mixer(
                _s5_out_kernel, "s5_out", [z, x],
                [s5_w_out[j].astype(bf16), s5_w_gate[j].astype(bf16)],
                ln_gain[i, 0].reshape(1, d), ln_bias[i, 0].reshape(1, d), router, alpha)
        else:
            new_shift_p.append(x[tp - bp:tp])
            new_shift_s.append(x[tp + ts - bs:tp + ts])
            x_prev = jnp.concatenate([jnp.zeros((bp, d), f32), x[:tp - bp],
                                      state_shift[j], x[tp:tp + ts - bs]], axis=0)
            row = lambda p: p.reshape(1, d)
            weights = [rw_mix[j],
                       rw_w_r[j].astype(bf16), rw_w_k[j].astype(bf16), rw_w_v[j].astype(bf16),
                       row(rw_w0[j]), _pad_cols(rw_w1[j]).astype(bf16), _pad_rows(rw_w2[j]).astype(bf16),
                       row(rw_a0[j]), _pad_cols(rw_a1[j]).astype(bf16), _pad_rows(rw_a2[j]).astype(bf16),
                       _pad_cols(rw_g1[j], MXU_DIM).astype(bf16), _pad_rows(rw_g2[j], MXU_DIM).astype(bf16),
                       row(rw_k_k[j]), row(rw_k_a[j])]
            if j > 0:
                weights += [row(rw_v0[j - 1]), _pad_cols(rw_v1[j - 1]).astype(bf16),
                            _pad_rows(rw_v2[j - 1]).astype(bf16)]
            r, w, k, v, kk, a, g = _rw_proj(x, x_prev, v_first if j > 0 else None, weights)
            if j == 0:
                v_first = v

            def run_group(row0, n_steps, bn, s0):
                sl = lambda y: _to_scan(y[row0:row0 + n_steps * bn], n_steps, bn)
                o, s_fin = _wkv_scan(sl(r), sl(w), sl(k), sl(v), sl(kk), sl(a), s0,
                                     _lanes_param(rw_lnx_g[j], bn), _lanes_param(rw_lnx_b[j], bn),
                                     _lanes_param(rw_r_k[j], bn))
                s_fin = jnp.transpose(s_fin.reshape(HEAD_DIM, HEAD_DIM, bn, heads), (2, 3, 1, 0))
                return _from_scan(o, n_steps, bn), s_fin

            o_p, s_p = run_group(0, lp, bp, jnp.zeros((HEAD_DIM, HEAD_DIM, bp * heads), f32))
            s0_s = jnp.transpose(state_wkv[j], (3, 2, 0, 1)).reshape(HEAD_DIM, HEAD_DIM, bs * heads)
            o_s, s_s = run_group(tp, ls, bs, s0_s)
            new_wkv_p.append(s_p)
            new_wkv_s.append(s_s)
            o = jnp.concatenate([o_p, o_s], axis=0)
            x1, ri, rf, cnt = _post_mixer(
                _rw_out_kernel, "rwkv_out", [o, g, x], [rw_w_o[j].astype(bf16)],
                ln_gain[i, 0].reshape(1, d), ln_bias[i, 0].reshape(1, d), router, alpha)

        x = _moe(x1, ri, rf, cnt, moe_w_gate[i], moe_w_up[i], moe_w_down[i],
                 ln_gain[i, 1].reshape(1, d), ln_bias[i, 1].reshape(1, d), alpha)

    y_prompt = jnp.transpose(x[:tp].reshape(lp, bp, d), (1, 0, 2))
    y_sample = jnp.transpose(x[tp:].reshape(ls, bs, d), (1, 0, 2))
    return (y_prompt, y_sample,
            jnp.stack(new_re_p), jnp.stack(new_im_p), jnp.stack(new_wkv_p), jnp.stack(new_shift_p),
            jnp.stack(new_re_s), jnp.stack(new_im_s), jnp.stack(new_wkv_s), jnp.stack(new_shift_s))
```

```python
import functools
import math

import jax
import jax.numpy as jnp
from jax import lax
from jax.experimental import pallas as pl
from jax.experimental.pallas import tpu as pltpu

f32 = jnp.float32
bf16 = jnp.bfloat16
i32 = jnp.int32

S5_GROUP = 16
S5_STATE = 64
HEAD_DIM = 64
GN_EPS = 64e-5
LN_EPS = 1e-5
N_EXPERTS = 16
N_EXPERT_GROUPS = 4
EXPERTS_PER_GROUP = N_EXPERTS // N_EXPERT_GROUPS

SUBLANES = 8
LANES = 128
MXU_DIM = 256
VMEM_LIMIT = 56 * 1024 * 1024

TOK_TILE = 512
PROJ_TILE = 256
S5_ROWS = 256
EXPERT_TILE = 256
COMBINE_TILE = 256
DISPATCH_TILE = 512
WKV_STEPS = 16
DMA_UNROLL = 8


def _cparams(n_axes=1, **kw):
    return pltpu.CompilerParams(dimension_semantics=("arbitrary",) * n_axes,
                                vmem_limit_bytes=VMEM_LIMIT, **kw)


def _bdot(a, w):
    return jnp.dot(a.astype(bf16), w, preferred_element_type=f32)


def _layer_norm(y, g, b):
    mu = jnp.mean(y, axis=-1, keepdims=True)
    d = y - mu
    var = jnp.mean(d * d, axis=-1, keepdims=True)
    return d * lax.rsqrt(var + LN_EPS) * g + b


def _aligned(idx):
    return idx if isinstance(idx, int) else pl.multiple_of(idx, SUBLANES)


def _softplus(x):
    return jnp.maximum(x, 0.0) + jnp.log1p(jnp.exp(-jnp.abs(x)))


def _route(x1, wr_hi, wr_lo, rbias, tri, cnt_ref, ri_ref, rf_ref, cnt_out_ref):
    tm = x1.shape[0]
    xh = x1.astype(bf16)
    xl = (x1 - xh.astype(f32)).astype(bf16)
    dn = (((1,), (1,)), ((), ()))
    logits = (lax.dot_general(wr_hi, xh, dn, preferred_element_type=f32)
              + lax.dot_general(wr_lo, xh, dn, preferred_element_type=f32)
              + lax.dot_general(wr_hi, xl, dn, preferred_element_type=f32))
    m = jnp.max(logits, axis=0, keepdims=True)
    ex = jnp.exp(logits - m)
    probs = ex / jnp.sum(ex, axis=0, keepdims=True)
    sel = probs + rbias
    s = [sel[e:e + 1, :] for e in range(N_EXPERTS)]
    p = [probs[e:e + 1, :] for e in range(N_EXPERTS)]

    best = None
    gi = None
    for g in range(N_EXPERT_GROUPS):
        a, b, c, d = s[4 * g:4 * g + 4]
        hi1, lo1 = jnp.maximum(a, b), jnp.minimum(a, b)
        hi2, lo2 = jnp.maximum(c, d), jnp.minimum(c, d)
        score = jnp.maximum(hi1, hi2) + jnp.maximum(jnp.minimum(hi1, hi2), jnp.maximum(lo1, lo2))
        if g == 0:
            best, gi = score, jnp.zeros(score.shape, i32)
        else:
            better = score > best
            best = jnp.where(better, score, best)
            gi = jnp.where(better, g, gi)

    def in_group(rows, j):
        out = rows[12 + j]
        for g in (2, 1, 0):
            out = jnp.where(gi == g, rows[4 * g + j], out)
        return out

    v = [in_group(s, j) for j in range(EXPERTS_PER_GROUP)]
    pv = [in_group(p, j) for j in range(EXPERTS_PER_GROUP)]
    order = []
    for j in range(EXPERTS_PER_GROUP):
        r = jnp.zeros(v[j].shape, i32)
        for i in range(EXPERTS_PER_GROUP):
            if i < j:
                r = r + jnp.where(v[i] >= v[j], 1, 0)
            elif i > j:
                r = r + jnp.where(v[i] > v[j], 1, 0)
        order.append(r)
    j1 = sum(jnp.where(order[j] == 0, j, 0) for j in range(EXPERTS_PER_GROUP))
    j2 = sum(jnp.where(order[j] == 1, j, 0) for j in range(EXPERTS_PER_GROUP))
    p1 = sum(jnp.where(order[j] == 0, pv[j], 0.0) for j in range(EXPERTS_PER_GROUP))
    p2 = sum(jnp.where(order[j] == 1, pv[j], 0.0) for j in range(EXPERTS_PER_GROUP))
    e1 = gi * EXPERTS_PER_GROUP + j1
    e2 = gi * EXPERTS_PER_GROUP + j2
    den = p1 + p2
    g1 = p1 / den
    g2 = p2 / den

    eio = lax.broadcasted_iota(i32, (N_EXPERTS, tm), 0)
    oh1 = eio == e1
    oh2 = eio == e2
    oh = jnp.where(oh1, 1.0, 0.0) + jnp.where(oh2, 1.0, 0.0)
    before = jnp.dot(oh.astype(bf16), tri, preferred_element_type=f32)
    before = before + cnt_ref[:, 0:1]
    r1 = jnp.sum(jnp.where(oh1, before, 0.0), axis=0, keepdims=True)
    r2 = jnp.sum(jnp.where(oh2, before, 0.0), axis=0, keepdims=True)
    cnt_new = cnt_ref[...] + jnp.sum(oh, axis=1, keepdims=True)
    cnt_ref[...] = cnt_new
    cnt_out_ref[...] = cnt_new

    ri_ref[...] = jnp.zeros(ri_ref.shape, i32)
    rf_ref[...] = jnp.zeros(rf_ref.shape, f32)
    ri_ref[0:1, :] = e1
    ri_ref[1:2, :] = e2
    ri_ref[2:3, :] = r1.astype(i32)
    ri_ref[3:4, :] = r2.astype(i32)
    rf_ref[0:1, :] = g1
    rf_ref[1:2, :] = g2


def _s5_disc_kernel(lre_ref, lim_ref, lstep_ref, bre_ref, bim_ref,
                    are_ref, aim_ref, bbre_ref, bbim_ref):
    lre = lre_ref[...]
    lim = lim_ref[...]
    step = jnp.exp(lstep_ref[...])
    mag = jnp.exp(lre * step)
    th = lim * step
    a_re = mag * jnp.cos(th)
    a_im = mag * jnp.sin(th)
    den = lre * lre + lim * lim
    q_re = ((a_re - 1.0) * lre + a_im * lim) / den
    q_im = (a_im * lre - (a_re - 1.0) * lim) / den
    are_ref[...] = a_re
    aim_ref[...] = a_im
    b_re = bre_ref[...]
    b_im = bim_ref[...]
    bbre_ref[...] = q_re[None] * b_re - q_im[None] * b_im
    bbim_ref[...] = q_re[None] * b_im + q_im[None] * b_re


def _s5_discretise(lam_re, lam_im, log_step, b_re, b_im):
    g, p = lam_re.shape
    c = b_re.shape[-1]
    outs = pl.pallas_call(
        _s5_disc_kernel,
        out_shape=(jax.ShapeDtypeStruct((g, p), f32), jax.ShapeDtypeStruct((g, p), f32),
                   jax.ShapeDtypeStruct((c, g, p), f32), jax.ShapeDtypeStruct((c, g, p), f32)),
        name="s5_discretise",
    )(lam_re, lam_im, log_step.reshape(g, 1),
      jnp.transpose(b_re, (2, 0, 1)), jnp.transpose(b_im, (2, 0, 1)))
    return outs


def _block_diag(w, n_blk):
    s, _, a, b = w.shape
    eye = jnp.eye(n_blk, dtype=w.dtype)
    out = w[:, :, :, None, :] * eye[None, :, None, :, None]
    return out.reshape(s, n_blk * a, n_blk * b)


def _s5_kernel(u_ref, h0re_ref, h0im_ref, are_ref, aim_ref, wb_ref, wcre_ref, wcim_ref, d_ref,
               z_ref, hre_ref, him_ref, bu_ref, *, bn, tb):
    i = pl.program_id(0)
    half = are_ref.shape[1]
    n_slab = wb_ref.shape[0]
    kw = wb_ref.shape[1]
    sw = half // n_slab
    cw = 1024

    @pl.when(i == 0)
    def _():
        hre_ref[...] = h0re_ref[...]
        him_ref[...] = h0im_ref[...]

    ub = u_ref[...].astype(bf16)
    for s in range(n_slab):
        res = jnp.dot(ub[:, s * kw:(s + 1) * kw], wb_ref[s], preferred_element_type=f32)
        bu_ref[:, s * sw:(s + 1) * sw] = res[:, :sw]
        bu_ref[:, half + s * sw:half + (s + 1) * sw] = res[:, sw:]

    def advance(t, first):
        def sub(j, carry):
            rr = _aligned(t * bn + j * SUBLANES)
            for c0 in range(0, half, cw):
                if first:
                    jr = _aligned(j * SUBLANES)
                    p_re = hre_ref[pl.ds(jr, SUBLANES), c0:c0 + cw]
                    p_im = him_ref[pl.ds(jr, SUBLANES), c0:c0 + cw]
                else:
                    pr = _aligned(rr - bn)
                    p_re = bu_ref[pl.ds(pr, SUBLANES), c0:c0 + cw]
                    p_im = bu_ref[pl.ds(pr, SUBLANES), half + c0:half + c0 + cw]
                a_re = are_ref[:, c0:c0 + cw]
                a_im = aim_ref[:, c0:c0 + cw]
                n_re = a_re * p_re - a_im * p_im + bu_ref[pl.ds(rr, SUBLANES), c0:c0 + cw]
                n_im = a_re * p_im + a_im * p_re + bu_ref[pl.ds(rr, SUBLANES), half + c0:half + c0 + cw]
                bu_ref[pl.ds(rr, SUBLANES), c0:c0 + cw] = n_re
                bu_ref[pl.ds(rr, SUBLANES), half + c0:half + c0 + cw] = n_im
            return carry
        if bn == SUBLANES:
            sub(0, 0)
        else:
            lax.fori_loop(0, bn // SUBLANES, sub, 0)

    advance(0, True)
    if tb > 1:
        def body(t, carry):
            advance(t, False)
            return carry
        lax.fori_loop(1, tb, body, 0)
    hre_ref[...] = bu_ref[(tb - 1) * bn:tb * bn, 0:half]
    him_ref[...] = bu_ref[(tb - 1) * bn:tb * bn, half:2 * half]

    nw = wcre_ref.shape[2]
    for n in range(wcre_ref.shape[0]):
        h_re = bu_ref[:, n * sw:(n + 1) * sw]
        h_im = bu_ref[:, half + n * sw:half + (n + 1) * sw]
        y = _bdot(h_re, wcre_ref[n]) - _bdot(h_im, wcim_ref[n])
        y = y + d_ref[:, n * nw:(n + 1) * nw] * u_ref[:, n * nw:(n + 1) * nw]
        z_ref[:, n * nw:(n + 1) * nw] = jax.nn.gelu(y, approximate=True).astype(z_ref.dtype)


def _s5_scan(x, row0, n_steps, bn, h0_re, h0_im, a_re8, a_im8, wb, wc_re, wc_im, d_skip):
    d = x.shape[1]
    tb = max(1, S5_ROWS // bn)
    rows = tb * bn
    assert n_steps % tb == 0 and row0 % rows == 0
    half = a_re8.shape[1]
    blk0 = row0 // rows
    const2 = lambda i: (0, 0)
    const3 = lambda i: (0, 0, 0)
    z, h_re, h_im = pl.pallas_call(
        functools.partial(_s5_kernel, bn=bn, tb=tb),
        grid=(n_steps // tb,),
        in_specs=[
            pl.BlockSpec((rows, d), lambda i: (i + blk0, 0)),
            pl.BlockSpec((bn, half), const2),
            pl.BlockSpec((bn, half), const2),
            pl.BlockSpec(a_re8.shape, const2),
            pl.BlockSpec(a_im8.shape, const2),
            pl.BlockSpec(wb.shape, const3),
            pl.BlockSpec(wc_re.shape, const3),
            pl.BlockSpec(wc_im.shape, const3),
            pl.BlockSpec((1, d), const2),
        ],
        out_specs=[
            pl.BlockSpec((rows, d), lambda i: (i, 0)),
            pl.BlockSpec((bn, half), const2),
            pl.BlockSpec((bn, half), const2),
        ],
        out_shape=(jax.ShapeDtypeStruct((n_steps * bn, d), bf16),
                   jax.ShapeDtypeStruct((bn, half), f32),
                   jax.ShapeDtypeStruct((bn, half), f32)),
        scratch_shapes=[pltpu.VMEM((rows, 2 * half), f32)],
        compiler_params=_cparams(1),
        name="s5_scan",
    )(x, h0_re, h0_im, a_re8, a_im8, wb, wc_re, wc_im, d_skip)
    return z, h_re, h_im


def _s5_out_kernel(zp_ref, zs_ref, x_ref, wo_ref, wg_ref, lng_ref, lnb_ref,
                   wrh_ref, wrl_ref, rb_ref, tri_ref,
                   x1_ref, ri_ref, rf_ref, cnt_out_ref, cnt_ref, *, alpha, n_ptiles):
    @pl.when(pl.program_id(0) == 0)
    def _():
        cnt_ref[...] = jnp.zeros(cnt_ref.shape, f32)

    z = jnp.where(pl.program_id(0) < n_ptiles, zp_ref[...], zs_ref[...])
    mixed = (jnp.dot(z, wo_ref[...], preferred_element_type=f32)
             * jax.nn.sigmoid(jnp.dot(z, wg_ref[...], preferred_element_type=f32)))
    x1 = _layer_norm(alpha * x_ref[...] + mixed, lng_ref[...], lnb_ref[...])
    x1_ref[...] = x1
    _route(x1, wrh_ref[...], wrl_ref[...], rb_ref[...], tri_ref[...], cnt_ref, ri_ref, rf_ref,
           cnt_out_ref)


def _rw_out_kernel(op_ref, os_ref, g_ref, x_ref, wo_ref, lng_ref, lnb_ref,
                   wrh_ref, wrl_ref, rb_ref, tri_ref,
                   x1_ref, ri_ref, rf_ref, cnt_out_ref, cnt_ref, *, alpha, n_ptiles):
    @pl.when(pl.program_id(0) == 0)
    def _():
        cnt_ref[...] = jnp.zeros(cnt_ref.shape, f32)

    o = jnp.where(pl.program_id(0) < n_ptiles, op_ref[...], os_ref[...])
    mixed_t = jnp.dot(wo_ref[...], (o * g_ref[...]).astype(bf16), preferred_element_type=f32)
    mixed = jnp.transpose(mixed_t)
    x1 = _layer_norm(alpha * x_ref[...] + mixed, lng_ref[...], lnb_ref[...])
    x1_ref[...] = x1
    _route(x1, wrh_ref[...], wrl_ref[...], rb_ref[...], tri_ref[...], cnt_ref, ri_ref, rf_ref,
           cnt_out_ref)


def _post_mixer(kernel_fn, name, mix_p, mix_s, tok_inputs, weights, lng, lnb, router, alpha,
                channel_major=False):
    t, d = tok_inputs[-1].shape
    tm = TOK_TILE
    tok_axis = 1 if channel_major else 0
    assert t % tm == 0 and mix_p.shape[tok_axis] % tm == 0 and mix_s.shape[tok_axis] % tm == 0
    n_ptiles = mix_p.shape[tok_axis] // tm
    wr_hi, wr_lo, rbias, tri = router
    tile = lambda i: (i, 0)
    const = lambda i: (0, 0)
    if channel_major:
        blk = (d, tm)
        at = lambda f: (lambda i: (0, f(i)))
    else:
        blk = (tm, d)
        at = lambda f: (lambda i: (f(i), 0))
    in_specs = ([pl.BlockSpec(blk, at(lambda i: jnp.minimum(i, n_ptiles - 1))),
                 pl.BlockSpec(blk, at(lambda i: jnp.maximum(i - n_ptiles, 0)))]
                + [pl.BlockSpec(blk, at(lambda i: i)) for _ in tok_inputs[:-1]]
                + [pl.BlockSpec((tm, d), tile)]
                + [pl.BlockSpec(w.shape, const) for w in weights]
                + [pl.BlockSpec((1, d), const), pl.BlockSpec((1, d), const),
                   pl.BlockSpec(wr_hi.shape, const), pl.BlockSpec(wr_lo.shape, const),
                   pl.BlockSpec(rbias.shape, const), pl.BlockSpec(tri.shape, const)])
    x1, ri, rf, cnt = pl.pallas_call(
        functools.partial(kernel_fn, alpha=alpha, n_ptiles=n_ptiles),
        grid=(t // tm,),
        in_specs=in_specs,
        out_specs=[pl.BlockSpec((tm, d), tile),
                   pl.BlockSpec((SUBLANES, tm), lambda i: (0, i)),
                   pl.BlockSpec((SUBLANES, tm), lambda i: (0, i)),
                   pl.BlockSpec((N_EXPERTS, LANES), const)],
        out_shape=(jax.ShapeDtypeStruct((t, d), f32),
                   jax.ShapeDtypeStruct((SUBLANES, t), i32),
                   jax.ShapeDtypeStruct((SUBLANES, t), f32),
                   jax.ShapeDtypeStruct((N_EXPERTS, LANES), f32)),
        scratch_shapes=[pltpu.VMEM((N_EXPERTS, LANES), f32)],
        compiler_params=_cparams(1),
        name=name,
    )(mix_p, mix_s, *tok_inputs, *weights, lng, lnb, wr_hi, wr_lo, rbias, tri)
    return x1, ri, rf, cnt


def _dispatch_kernel(d1_ref, d2_ref, x_ref, xs_in_hbm, xs_hbm, sem, *, tile):
    del xs_in_hbm
    base = pl.program_id(0) * tile

    def row_copy(j, dst):
        return pltpu.make_async_copy(x_ref.at[pl.ds(j, 1)], xs_hbm.at[pl.ds(dst, 1)], sem)

    def start(jq, carry):
        for u in range(DMA_UNROLL):
            j = jq * DMA_UNROLL + u
            row_copy(j, d1_ref[base + j]).start()
            row_copy(j, d2_ref[base + j]).start()
        return carry
    lax.fori_loop(0, tile // DMA_UNROLL, start, 0)

    def wait(jq, carry):
        for _ in range(2 * DMA_UNROLL):
            row_copy(0, 0).wait()
        return carry
    lax.fori_loop(0, tile // DMA_UNROLL, wait, 0)


def _dispatch(x1, d1, d2, n_rows):
    t, d = x1.shape
    tile = DISPATCH_TILE
    assert t % tile == 0
    xs0 = jnp.zeros((n_rows, d), x1.dtype)
    return pl.pallas_call(
        functools.partial(_dispatch_kernel, tile=tile),
        grid_spec=pltpu.PrefetchScalarGridSpec(
            num_scalar_prefetch=2,
            grid=(t // tile,),
            in_specs=[pl.BlockSpec((tile, d), lambda i, a, b: (i, 0)),
                      pl.BlockSpec(memory_space=pl.ANY)],
            out_specs=pl.BlockSpec(memory_space=pl.ANY),
            scratch_shapes=[pltpu.SemaphoreType.DMA(())],
        ),
        out_shape=jax.ShapeDtypeStruct((n_rows, d), x1.dtype),
        input_output_aliases={3: 0},
        compiler_params=_cparams(1, has_side_effects=True),
        name="moe_dispatch",
    )(d1, d2, x1, xs0)


def _expert_kernel(te_ref, nu_ref, xs_ref, wg_ref, wu_ref, wd_ref, ys_ref, wgb, wub, wdb):
    i = pl.program_id(0)
    used = i < nu_ref[0]
    prev = te_ref[jnp.maximum(i - 1, 0)]
    fresh = jnp.logical_or(i == 0, te_ref[i] != prev)

    @pl.when(jnp.logical_and(used, fresh))
    def _():
        wgb[...] = wg_ref[0].astype(bf16)
        wub[...] = wu_ref[0].astype(bf16)
        wdb[...] = wd_ref[0].astype(bf16)

    @pl.when(used)
    def _():
        x = xs_ref[...].astype(bf16)
        h = (jax.nn.silu(jnp.dot(x, wgb[...], preferred_element_type=f32))
             * jnp.dot(x, wub[...], preferred_element_type=f32))
        ys_ref[...] = jnp.dot(h.astype(bf16), wdb[...], preferred_element_type=f32)

    @pl.when(jnp.logical_not(used))
    def _():
        ys_ref[...] = jnp.zeros(ys_ref.shape, ys_ref.dtype)


def _experts(xs, tile_expert, n_used, wg, wu, wd):
    n_rows, d = xs.shape
    tile = EXPERT_TILE
    n_tiles = n_rows // tile
    dff = wg.shape[2]

    def row_map(i, te, nu):
        return (jnp.minimum(i, nu[0] - 1), 0)

    def w_map(i, te, nu):
        return (te[i], 0, 0)

    return pl.pallas_call(
        _expert_kernel,
        grid_spec=pltpu.PrefetchScalarGridSpec(
            num_scalar_prefetch=2,
            grid=(n_tiles,),
            in_specs=[pl.BlockSpec((tile, d), row_map),
                      pl.BlockSpec((1, d, dff), w_map),
                      pl.BlockSpec((1, d, dff), w_map),
                      pl.BlockSpec((1, dff, d), w_map)],
            out_specs=pl.BlockSpec((tile, d), lambda i, te, nu: (i, 0)),
            scratch_shapes=[pltpu.VMEM((d, dff), bf16), pltpu.VMEM((d, dff), bf16),
                            pltpu.VMEM((dff, d), bf16)],
        ),
        out_shape=jax.ShapeDtypeStruct((n_rows, d), f32),
        compiler_params=_cparams(1),
        name="moe_experts",
    )(tile_expert, n_used, xs, wg, wu, wd)


def _combine_kernel(d1_ref, d2_ref, ys_hbm, x1_ref, gate_ref, lng_ref, lnb_ref, out_ref,
                    y1_buf, y2_buf, sem, *, tile, alpha):
    base = pl.program_id(0) * tile

    def row_copy(src, buf, j):
        return pltpu.make_async_copy(ys_hbm.at[pl.ds(src, 1)], buf.at[pl.ds(j, 1)], sem)

    def start(jq, carry):
        for u in range(DMA_UNROLL):
            j = jq * DMA_UNROLL + u
            row_copy(d1_ref[base + j], y1_buf, j).start()
            row_copy(d2_ref[base + j], y2_buf, j).start()
        return carry
    lax.fori_loop(0, tile // DMA_UNROLL, start, 0)

    def wait(jq, carry):
        for _ in range(DMA_UNROLL):
            row_copy(0, y1_buf, 0).wait()
            row_copy(0, y2_buf, 0).wait()
        return carry
    lax.fori_loop(0, tile // DMA_UNROLL, wait, 0)

    gates = gate_ref[...]
    moe = gates[:, 0:1] * y1_buf[...] + gates[:, 1:2] * y2_buf[...]
    out_ref[...] = _layer_norm(alpha * x1_ref[...] + moe, lng_ref[...], lnb_ref[...])


def _combine(ys, x1, d1, d2, gates, lng, lnb, alpha):
    t, d = x1.shape
    tile = COMBINE_TILE
    assert t % tile == 0
    tok = lambda i, a, b: (i, 0)
    const = lambda i, a, b: (0, 0)
    return pl.pallas_call(
        functools.partial(_combine_kernel, tile=tile, alpha=alpha),
        grid_spec=pltpu.PrefetchScalarGridSpec(
            num_scalar_prefetch=2,
            grid=(t // tile,),
            in_specs=[pl.BlockSpec(memory_space=pl.ANY),
                      pl.BlockSpec((tile, d), tok),
                      pl.BlockSpec((tile, 2), tok),
                      pl.BlockSpec((1, d), const),
                      pl.BlockSpec((1, d), const)],
            out_specs=pl.BlockSpec((tile, d), tok),
            scratch_shapes=[pltpu.VMEM((tile, d), f32), pltpu.VMEM((tile, d), f32),
                            pltpu.SemaphoreType.DMA(())],
        ),
        out_shape=jax.ShapeDtypeStruct((t, d), f32),
        compiler_params=_cparams(1),
        name="moe_combine",
    )(d1, d2, ys, x1, gates, lng, lnb)


def _moe(x1, ri, rf, cnt, wg, wu, wd, lng, lnb, alpha):
    t, d = x1.shape
    tile = EXPERT_TILE
    n_tiles = -(-2 * t // tile) + N_EXPERTS
    counts = cnt[:, 0].astype(i32)
    padded = ((counts + tile - 1) // tile) * tile
    ends = jnp.cumsum(padded)
    starts = ends - padded
    n_used = (ends[-1] // tile).astype(i32).reshape(1)
    d1 = starts[ri[0]] + ri[2]
    d2 = starts[ri[1]] + ri[3]
    tile_start = jnp.minimum(jnp.arange(n_tiles, dtype=i32), n_used[0] - 1) * tile
    tile_expert = jnp.sum(tile_start[:, None] >= ends[None, :], axis=1).astype(i32)
    xs = _dispatch(x1, d1, d2, n_tiles * tile)
    ys = _experts(xs, tile_expert, n_used, wg, wu, wd)
    gates = jnp.transpose(rf[0:2])
    return _combine(ys, x1, d1, d2, gates, lng, lnb, alpha)


def _rw_proj_kernel(*refs, has_vres, n_ptiles):
    if has_vres:
        (x_ref, hp_ref, hs_ref, sh_ref, vf_ref, mix_ref, wr_ref, wk_ref, wv_ref, w0_ref, w1_ref,
         w2_ref, a0_ref, a1_ref, a2_ref, g1_ref, g2_ref, kk_ref, ka_ref, v0_ref, v1_ref, v2_ref,
         r_out, w_out, k_out, v_out, kk_out, a_out, g_out, xp_buf) = refs
    else:
        (x_ref, hp_ref, hs_ref, sh_ref, mix_ref, wr_ref, wk_ref, wv_ref, w0_ref, w1_ref, w2_ref,
         a0_ref, a1_ref, a2_ref, g1_ref, g2_ref, kk_ref, ka_ref,
         r_out, w_out, k_out, v_out, kk_out, a_out, g_out, xp_buf) = refs
    i = pl.program_id(0)
    tm = x_ref.shape[0]
    bp, bs = hp_ref.shape[0], hs_ref.shape[0]

    @pl.when(i < n_ptiles)
    def _():
        xp_buf[0:bp, :] = jnp.where(i == 0, 0.0, hp_ref[...])
        if tm > bp:
            xp_buf[bp:tm, :] = x_ref[0:tm - bp, :]

    @pl.when(i >= n_ptiles)
    def _():
        xp_buf[0:bs, :] = jnp.where(i == n_ptiles, sh_ref[...], hs_ref[...])
        if tm > bs:
            xp_buf[bs:tm, :] = x_ref[0:tm - bs, :]

    x = x_ref[...]
    xx = xp_buf[...] - x

    def mixed(i):
        return (x + xx * mix_ref[i:i + 1, :]).astype(bf16)

    def proj(wt_ref, xm):
        return lax.dot_general(wt_ref[...], xm, (((1,), (1,)), ((), ())),
                               preferred_element_type=f32)

    def wdot(wt_ref, y):
        return jnp.dot(wt_ref[...], y.astype(bf16), preferred_element_type=f32)

    def col(p_ref):
        return jnp.tile(p_ref[...], (1, tm // LANES))

    r_out[...] = proj(wr_ref, mixed(0))
    lora_w = jnp.tanh(proj(w1_ref, mixed(1)))
    w = -_softplus(-(col(w0_ref) + wdot(w2_ref, lora_w))) - 0.5
    w_out[...] = jnp.exp(-jnp.exp(w))
    k = proj(wk_ref, mixed(2))
    xv = mixed(3)
    v = proj(wv_ref, xv)
    a = jax.nn.sigmoid(col(a0_ref) + wdot(a2_ref, proj(a1_ref, mixed(4))))
    a_out[...] = a
    gate = jax.nn.sigmoid(proj(g1_ref, mixed(5)))
    g_out[...] = wdot(g2_ref, gate)
    kk_out[...] = k * col(kk_ref)
    k_out[...] = k * (1.0 + (a - 1.0) * col(ka_ref))
    if has_vres:
        lora_v = wdot(v2_ref, proj(v1_ref, xv))
        v = v + (vf_ref[...] - v) * jax.nn.sigmoid(col(v0_ref) + lora_v)
    v_out[...] = v


def _rw_proj(x, tp, bp, bs, shift_s, v_first, weights):
    t, d = x.shape
    tm = PROJ_TILE
    assert t % tm == 0 and tp % tm == 0 and tm % bp == 0 and tm % bs == 0
    n_ptiles = tp // tm
    has_vres = v_first is not None
    tile = lambda i: (i, 0)
    const = lambda i: (0, 0)
    halo_p = lambda i: (jnp.maximum(jnp.minimum(i, n_ptiles - 1) * (tm // bp) - 1, 0), 0)
    halo_s = lambda i: (jnp.maximum(i, n_ptiles) * (tm // bs) - 1, 0)
    cols = lambda i: (0, i)
    tok = [x, x, x, shift_s] + ([v_first] if has_vres else [])
    tok_specs = [pl.BlockSpec((tm, d), tile), pl.BlockSpec((bp, d), halo_p),
                 pl.BlockSpec((bs, d), halo_s), pl.BlockSpec((bs, d), const)]
    if has_vres:
        tok_specs.append(pl.BlockSpec((d, tm), cols))
    outs = pl.pallas_call(
        functools.partial(_rw_proj_kernel, has_vres=has_vres, n_ptiles=n_ptiles),
        grid=(t // tm,),
        in_specs=tok_specs + [pl.BlockSpec(w.shape, const) for w in weights],
        out_specs=[pl.BlockSpec((d, tm), cols) for _ in range(7)],
        out_shape=tuple(jax.ShapeDtypeStruct((d, t), f32) for _ in range(7)),
        scratch_shapes=[pltpu.VMEM((tm, d), f32)],
        compiler_params=_cparams(1),
        name="rwkv_proj",
    )(*tok, *weights)
    return outs


def _block_transpose(buf):
    nb = buf.shape[0]
    bw = LANES // nb
    blk = lax.broadcasted_iota(i32, tuple(buf.shape[1:]), 1) // bw
    s = nb // 2
    while s >= 1:
        low = (blk & s) == 0
        for i in range(nb):
            if i & s == 0:
                a = buf[i]
                b = buf[i + s]
                buf[i] = jnp.where(low, a, pltpu.roll(b, s * bw, 1))
                buf[i + s] = jnp.where(low, pltpu.roll(a, LANES - s * bw, 1), b)
        s //= 2


def _wkv_kernel(r_ref, w_ref, k_ref, v_ref, kk_ref, a_ref, s0_ref, lng_ref, lnb_ref, rk_ref,
                o_ref, s_ref, in_buf, o_buf, *, steps, time_on_lanes):
    n = s_ref.shape[0]
    R, W, K, V, KK, A = range(6)

    @pl.when(pl.program_id(1) == 0)
    def _():
        s_ref[...] = s0_ref[...]

    for q, ref in enumerate((r_ref, w_ref, k_ref, v_ref, kk_ref, a_ref)):
        if time_on_lanes:
            in_buf[q] = ref[...]
            _block_transpose(in_buf.at[q])
        else:
            for t in range(steps):
                in_buf[q, t] = ref[0, :, t * LANES:(t + 1) * LANES]

    def prep(t, carry):
        kk_raw = in_buf[KK, t]
        norm = jnp.sqrt(jnp.sum(kk_raw * kk_raw, axis=0, keepdims=True))
        kk = kk_raw / jnp.maximum(norm, 1e-12)
        in_buf[KK, t] = -kk
        in_buf[A, t] = kk * in_buf[A, t]
        return carry
    lax.fori_loop(0, steps, prep, 0)

    def bcast(q, t, kidx):
        row = in_buf[q, t, pl.ds(kidx, 1), :]
        return jnp.broadcast_to(row, (n, LANES))

    zeros = jnp.zeros((n, LANES), f32)

    def state_dot(t):
        def body(kq, sa):
            for u in range(SUBLANES):
                kidx = kq * SUBLANES + u
                sa = sa + s_ref[kidx] * bcast(KK, t, kidx)
            return sa
        return lax.fori_loop(0, n // SUBLANES, body, zeros)

    def advance(t, sa, look_ahead):
        vv = in_buf[V, t]

        def body(kq, carry):
            out, sa_next = carry
            for u in range(SUBLANES):
                kidx = kq * SUBLANES + u
                s_new = (s_ref[kidx] * bcast(W, t, kidx) + sa * bcast(A, t, kidx)
                         + vv * bcast(K, t, kidx))
                s_ref[kidx] = s_new
                out = out + s_new * bcast(R, t, kidx)
                if look_ahead:
                    sa_next = sa_next + s_new * bcast(KK, t + 1, kidx)
            return out, sa_next
        out, sa_next = lax.fori_loop(0, n // SUBLANES, body, (zeros, zeros))

        mu = jnp.mean(out, axis=0, keepdims=True)
        dlt = out - mu
        var = jnp.mean(dlt * dlt, axis=0, keepdims=True)
        o_n = dlt * lax.rsqrt(var + GN_EPS) * lng_ref[...] + lnb_ref[...]
        bonus = jnp.sum(in_buf[R, t] * in_buf[K, t] * rk_ref[...], axis=0, keepdims=True) * vv
        o_buf[t] = o_n + bonus
        return sa_next

    sa = state_dot(0)
    if steps > 1:
        sa = lax.fori_loop(0, steps - 1, lambda t, s: advance(t, s, True), sa)
    advance(steps - 1, sa, False)

    if time_on_lanes:
        _block_transpose(o_buf)
        o_ref[...] = o_buf[...]
    else:
        for t in range(steps):
            o_ref[0, :, t * LANES:(t + 1) * LANES] = o_buf[t]


def _wkv_scan(streams, col0, n_steps, bn, s0, lng, lnb, rk):
    d = streams[0].shape[0]
    n = HEAD_DIM
    heads = d // n
    time_on_lanes = bn * heads == LANES
    if time_on_lanes:
        steps = LANES // bn
        assert n_steps % steps == 0 and col0 % LANES == 0
        grid = (1, n_steps // steps)
        blk = (heads, n, LANES)
        c0 = col0 // LANES
        in_map = lambda j, t: (0, 0, c0 + t)
        out_map = lambda j, t: (0, 0, t)
    else:
        steps = n_steps
        assert bn == LANES and col0 % (steps * LANES) == 0
        grid = (heads, 1)
        blk = (1, n, steps * LANES)
        c0 = col0 // (steps * LANES)
        in_map = lambda j, t: (j, 0, c0)
        out_map = lambda j, t: (j, 0, 0)
    per_lane = lambda j, t: (0, 0, j)
    par = lambda j, t: (0, j)
    views = [s.reshape(heads, n, s.shape[1]) for s in streams]
    o, s_fin = pl.pallas_call(
        functools.partial(_wkv_kernel, steps=steps, time_on_lanes=time_on_lanes),
        grid=grid,
        in_specs=([pl.BlockSpec(blk, in_map) for _ in range(6)]
                  + [pl.BlockSpec((n, n, LANES), per_lane)]
                  + [pl.BlockSpec((n, LANES), par) for _ in range(3)]),
        out_specs=[pl.BlockSpec(blk, out_map),
                   pl.BlockSpec((n, n, LANES), per_lane)],
        out_shape=(jax.ShapeDtypeStruct((heads, n, n_steps * bn), f32),
                   jax.ShapeDtypeStruct((n, n, heads * bn), f32)),
        scratch_shapes=[pltpu.VMEM((6, steps, n, LANES), f32), pltpu.VMEM((steps, n, LANES), f32)],
        compiler_params=_cparams(2),
        name="wkv_scan",
    )(*views, s0, lng, lnb, rk)
    return o.reshape(d, n_steps * bn), s_fin


def _lanes_param(p, bn):
    h = p.size // HEAD_DIM
    y = jnp.transpose(p.reshape(h, HEAD_DIM))
    return jnp.repeat(y, bn, axis=1)


def _col_param(p):
    return jnp.broadcast_to(p.reshape(-1, 1), (p.size, LANES))


def _pad_cols(w, mult=LANES):
    pad = (-w.shape[1]) % mult
    return jnp.pad(w, ((0, 0), (0, pad)))


def _pad_rows(w, mult=LANES):
    pad = (-w.shape[0]) % mult
    return jnp.pad(w, ((0, pad), (0, 0)))


def kernel(x_prompt, x_sample, state_s5_re, state_s5_im, state_wkv, state_shift, ln_gain, ln_bias,
           s5_lambda_re, s5_lambda_im, s5_log_step, s5_b_re, s5_b_im, s5_c_re, s5_c_im, s5_d,
           s5_w_out, s5_w_gate, rw_mix, rw_w_r, rw_w_k, rw_w_v, rw_w_o, rw_w0, rw_w1, rw_w2,
           rw_a0, rw_a1, rw_a2, rw_v0, rw_v1, rw_v2, rw_g1, rw_g2, rw_k_k, rw_k_a, rw_r_k,
           rw_lnx_g, rw_lnx_b, w_router, router_bias, moe_w_gate, moe_w_up, moe_w_down):
    bp, lp, d = x_prompt.shape
    bs, ls, _ = x_sample.shape
    tp, ts = bp * lp, bs * ls
    depth = ln_gain.shape[0]
    alpha = (2.0 * depth) ** 0.25
    n_groups = s5_lambda_re.shape[1]
    n_state = s5_lambda_re.shape[2]
    half = n_groups * n_state
    heads = d // HEAD_DIM

    x = jnp.concatenate([jnp.transpose(x_prompt, (1, 0, 2)).reshape(tp, d),
                         jnp.transpose(x_sample, (1, 0, 2)).reshape(ts, d)], axis=0)

    wr_t = jnp.transpose(w_router)
    wr_hi = wr_t.astype(bf16)
    wr_lo = (wr_t - wr_hi.astype(f32)).astype(bf16)
    rbias = router_bias.reshape(N_EXPERTS, 1)
    tri = jnp.triu(jnp.ones((TOK_TILE, TOK_TILE), bf16), k=1)
    router = (wr_hi, wr_lo, rbias, tri)

    new_re_p, new_im_p, new_re_s, new_im_s = [], [], [], []
    new_wkv_p, new_wkv_s, new_shift_p, new_shift_s = [], [], [], []
    v_first = None
    groups_per_slab = MXU_DIM // S5_GROUP

    for i in range(depth):
        j = i // 2
        if i % 2 == 0:
            a_re, a_im, bb_re, bb_im = _s5_discretise(s5_lambda_re[j], s5_lambda_im[j], s5_log_step[j],
                                                      s5_b_re[j], s5_b_im[j])
            a_re8 = jnp.broadcast_to(a_re.reshape(1, half), (SUBLANES, half))
            a_im8 = jnp.broadcast_to(a_im.reshape(1, half), (SUBLANES, half))
            n_slab = n_groups // groups_per_slab

            def pack_b(bb):
                w = jnp.transpose(bb, (1, 0, 2)).reshape(n_slab, groups_per_slab, S5_GROUP, n_state)
                return _block_diag(w, groups_per_slab)
            wb = jnp.concatenate([pack_b(bb_re), pack_b(bb_im)], axis=2).astype(bf16)

            def pack_c(c):
                w = jnp.transpose(c, (0, 2, 1)).reshape(n_slab, groups_per_slab, n_state, S5_GROUP)
                return _block_diag(w, groups_per_slab).astype(bf16)
            wc_re, wc_im = pack_c(s5_c_re[j]), pack_c(s5_c_im[j])
            d_skip = s5_d[j].reshape(1, d)

            zero_h = jnp.zeros((bp, half), f32)
            z_p, hre_p, him_p = _s5_scan(x, 0, lp, bp, zero_h, zero_h, a_re8, a_im8,
                                         wb, wc_re, wc_im, d_skip)
            z_s, hre_s, him_s = _s5_scan(x, tp, ls, bs, state_s5_re[j].reshape(bs, half),
                                         state_s5_im[j].reshape(bs, half), a_re8, a_im8,
                                         wb, wc_re, wc_im, d_skip)
            new_re_p.append(hre_p.reshape(bp, n_groups, n_state))
            new_im_p.append(him_p.reshape(bp, n_groups, n_state))
            new_re_s.append(hre_s.reshape(bs, n_groups, n_state))
            new_im_s.append(him_s.reshape(bs, n_groups, n_state))
            x1, ri, rf, cnt = _post_mixer(
                _s5_out_kernel, "s5_out", z_p, z_s, [x],
                [s5_w_out[j].astype(bf16), s5_w_gate[j].astype(bf16)],
                ln_gain[i, 0].reshape(1, d), ln_bias[i, 0].reshape(1, d), router, alpha)
        else:
            new_shift_p.append(x[tp - bp:tp])
            new_shift_s.append(x[tp + ts - bs:tp + ts])
            wt = lambda w: jnp.transpose(w).astype(bf16)
            weights = [rw_mix[j],
                       wt(rw_w_r[j]), wt(rw_w_k[j]), wt(rw_w_v[j]),
                       _col_param(rw_w0[j]), wt(_pad_cols(rw_w1[j])), wt(_pad_rows(rw_w2[j])),
                       _col_param(rw_a0[j]), wt(_pad_cols(rw_a1[j])), wt(_pad_rows(rw_a2[j])),
                       wt(_pad_cols(rw_g1[j], MXU_DIM)), wt(_pad_rows(rw_g2[j], MXU_DIM)),
                       _col_param(rw_k_k[j]), _col_param(rw_k_a[j])]
            if j > 0:
                weights += [_col_param(rw_v0[j - 1]), wt(_pad_cols(rw_v1[j - 1])),
                            wt(_pad_rows(rw_v2[j - 1]))]
            r, w, k, v, kk, a, g = _rw_proj(x, tp, bp, bs, state_shift[j],
                                            v_first if j > 0 else None, weights)
            if j == 0:
                v_first = v

            def run_group(col0, n_steps, bn, s0):
                o, s_fin = _wkv_scan((r, w, k, v, kk, a), col0, n_steps, bn, s0,
                                     _lanes_param(rw_lnx_g[j], bn), _lanes_param(rw_lnx_b[j], bn),
                                     _lanes_param(rw_r_k[j], bn))
                s_fin = jnp.transpose(s_fin.reshape(HEAD_DIM, HEAD_DIM, heads, bn), (3, 2, 1, 0))
                return o, s_fin

            o_p, s_p = run_group(0, lp, bp, jnp.zeros((HEAD_DIM, HEAD_DIM, heads * bp), f32))
            s0_s = jnp.transpose(state_wkv[j], (3, 2, 1, 0)).reshape(HEAD_DIM, HEAD_DIM, heads * bs)
            o_s, s_s = run_group(tp, ls, bs, s0_s)
            new_wkv_p.append(s_p)
            new_wkv_s.append(s_s)
            x1, ri, rf, cnt = _post_mixer(
                _rw_out_kernel, "rwkv_out", o_p, o_s, [g, x], [wt(rw_w_o[j])],
                ln_gain[i, 0].reshape(1, d), ln_bias[i, 0].reshape(1, d), router, alpha,
                channel_major=True)

        x = _moe(x1, ri, rf, cnt, moe_w_gate[i], moe_w_up[i], moe_w_down[i],
                 ln_gain[i, 1].reshape(1, d), ln_bias[i, 1].reshape(1, d), alpha)

    y_prompt = jnp.transpose(x[:tp].reshape(lp, bp, d), (1, 0, 2))
    y_sample = jnp.transpose(x[tp:].reshape(ls, bs, d), (1, 0, 2))
    return (y_prompt, y_sample,
            jnp.stack(new_re_p), jnp.stack(new_im_p), jnp.stack(new_wkv_p), jnp.stack(new_shift_p),
            jnp.stack(new_re_s), jnp.stack(new_im_s), jnp.stack(new_wkv_s), jnp.stack(new_shift_s))
```

```python
import functools
import math

import jax
import jax.numpy as jnp
from jax import lax
from jax.experimental import pallas as pl
from jax.experimental.pallas import tpu as pltpu

f32 = jnp.float32
bf16 = jnp.bfloat16
i32 = jnp.int32

S5_GROUP = 16
S5_STATE = 64
HEAD_DIM = 64
GN_EPS = 64e-5
LN_EPS = 1e-5
N_EXPERTS = 16
N_EXPERT_GROUPS = 4
EXPERTS_PER_GROUP = N_EXPERTS // N_EXPERT_GROUPS

SUBLANES = 8
LANES = 128
MXU_DIM = 256
VMEM_LIMIT = 56 * 1024 * 1024

TOK_TILE = 512
PROJ_TILE = 256
S5_ROWS = 256
EXPERT_TILE = 256
COMBINE_TILE = 256
DISPATCH_TILE = 512
WKV_STEPS = 16
DMA_UNROLL = 8


def _cparams(n_axes=1, **kw):
    return pltpu.CompilerParams(dimension_semantics=("arbitrary",) * n_axes,
                                vmem_limit_bytes=VMEM_LIMIT, **kw)


def _bdot(a, w):
    return jnp.dot(a.astype(bf16), w, preferred_element_type=f32)


def _layer_norm(y, g, b):
    mu = jnp.mean(y, axis=-1, keepdims=True)
    d = y - mu
    var = jnp.mean(d * d, axis=-1, keepdims=True)
    return d * lax.rsqrt(var + LN_EPS) * g + b


def _aligned(idx):
    return idx if isinstance(idx, int) else pl.multiple_of(idx, SUBLANES)


def _softplus(x):
    return jnp.maximum(x, 0.0) + jnp.log1p(jnp.exp(-jnp.abs(x)))


def _route(x1, wr_hi, wr_lo, rbias, tri, cnt_ref, ri_ref, rf_ref, cnt_out_ref):
    tm = x1.shape[0]
    xh = x1.astype(bf16)
    xl = (x1 - xh.astype(f32)).astype(bf16)
    dn = (((1,), (1,)), ((), ()))
    logits = (lax.dot_general(wr_hi, xh, dn, preferred_element_type=f32)
              + lax.dot_general(wr_lo, xh, dn, preferred_element_type=f32)
              + lax.dot_general(wr_hi, xl, dn, preferred_element_type=f32))
    m = jnp.max(logits, axis=0, keepdims=True)
    ex = jnp.exp(logits - m)
    probs = ex / jnp.sum(ex, axis=0, keepdims=True)
    sel = probs + rbias
    s = [sel[e:e + 1, :] for e in range(N_EXPERTS)]
    p = [probs[e:e + 1, :] for e in range(N_EXPERTS)]

    best = None
    gi = None
    for g in range(N_EXPERT_GROUPS):
        a, b, c, d = s[4 * g:4 * g + 4]
        hi1, lo1 = jnp.maximum(a, b), jnp.minimum(a, b)
        hi2, lo2 = jnp.maximum(c, d), jnp.minimum(c, d)
        score = jnp.maximum(hi1, hi2) + jnp.maximum(jnp.minimum(hi1, hi2), jnp.maximum(lo1, lo2))
        if g == 0:
            best, gi = score, jnp.zeros(score.shape, i32)
        else:
            better = score > best
            best = jnp.where(better, score, best)
            gi = jnp.where(better, g, gi)

    def in_group(rows, j):
        out = rows[12 + j]
        for g in (2, 1, 0):
            out = jnp.where(gi == g, rows[4 * g + j], out)
        return out

    v = [in_group(s, j) for j in range(EXPERTS_PER_GROUP)]
    pv = [in_group(p, j) for j in range(EXPERTS_PER_GROUP)]
    order = []
    for j in range(EXPERTS_PER_GROUP):
        r = jnp.zeros(v[j].shape, i32)
        for i in range(EXPERTS_PER_GROUP):
            if i < j:
                r = r + jnp.where(v[i] >= v[j], 1, 0)
            elif i > j:
                r = r + jnp.where(v[i] > v[j], 1, 0)
        order.append(r)
    j1 = sum(jnp.where(order[j] == 0, j, 0) for j in range(EXPERTS_PER_GROUP))
    j2 = sum(jnp.where(order[j] == 1, j, 0) for j in range(EXPERTS_PER_GROUP))
    p1 = sum(jnp.where(order[j] == 0, pv[j], 0.0) for j in range(EXPERTS_PER_GROUP))
    p2 = sum(jnp.where(order[j] == 1, pv[j], 0.0) for j in range(EXPERTS_PER_GROUP))
    e1 = gi * EXPERTS_PER_GROUP + j1
    e2 = gi * EXPERTS_PER_GROUP + j2
    den = p1 + p2
    g1 = p1 / den
    g2 = p2 / den

    eio = lax.broadcasted_iota(i32, (N_EXPERTS, tm), 0)
    oh1 = eio == e1
    oh2 = eio == e2
    oh = jnp.where(oh1, 1.0, 0.0) + jnp.where(oh2, 1.0, 0.0)
    before = jnp.dot(oh.astype(bf16), tri, preferred_element_type=f32)
    before = before + cnt_ref[:, 0:1]
    r1 = jnp.sum(jnp.where(oh1, before, 0.0), axis=0, keepdims=True)
    r2 = jnp.sum(jnp.where(oh2, before, 0.0), axis=0, keepdims=True)
    cnt_new = cnt_ref[...] + jnp.sum(oh, axis=1, keepdims=True)
    cnt_ref[...] = cnt_new
    cnt_out_ref[...] = cnt_new

    ri_ref[...] = jnp.zeros(ri_ref.shape, i32)
    rf_ref[...] = jnp.zeros(rf_ref.shape, f32)
    ri_ref[0:1, :] = e1
    ri_ref[1:2, :] = e2
    ri_ref[2:3, :] = r1.astype(i32)
    ri_ref[3:4, :] = r2.astype(i32)
    rf_ref[0:1, :] = g1
    rf_ref[1:2, :] = g2


def _s5_disc_kernel(lre_ref, lim_ref, lstep_ref, bre_ref, bim_ref,
                    are_ref, aim_ref, bbre_ref, bbim_ref):
    lre = lre_ref[...]
    lim = lim_ref[...]
    step = jnp.exp(lstep_ref[...])
    mag = jnp.exp(lre * step)
    th = lim * step
    a_re = mag * jnp.cos(th)
    a_im = mag * jnp.sin(th)
    den = lre * lre + lim * lim
    q_re = ((a_re - 1.0) * lre + a_im * lim) / den
    q_im = (a_im * lre - (a_re - 1.0) * lim) / den
    are_ref[...] = a_re
    aim_ref[...] = a_im
    b_re = bre_ref[...]
    b_im = bim_ref[...]
    bbre_ref[...] = q_re[None] * b_re - q_im[None] * b_im
    bbim_ref[...] = q_re[None] * b_im + q_im[None] * b_re


def _s5_discretise(lam_re, lam_im, log_step, b_re, b_im):
    g, p = lam_re.shape
    c = b_re.shape[-1]
    outs = pl.pallas_call(
        _s5_disc_kernel,
        out_shape=(jax.ShapeDtypeStruct((g, p), f32), jax.ShapeDtypeStruct((g, p), f32),
                   jax.ShapeDtypeStruct((c, g, p), f32), jax.ShapeDtypeStruct((c, g, p), f32)),
        name="s5_discretise",
    )(lam_re, lam_im, log_step.reshape(g, 1),
      jnp.transpose(b_re, (2, 0, 1)), jnp.transpose(b_im, (2, 0, 1)))
    return outs


def _block_diag(w, n_blk):
    s, _, a, b = w.shape
    eye = jnp.eye(n_blk, dtype=w.dtype)
    out = w[:, :, :, None, :] * eye[None, :, None, :, None]
    return out.reshape(s, n_blk * a, n_blk * b)


def _s5_kernel(u_ref, h0re_ref, h0im_ref, are_ref, aim_ref, wb_ref, wcre_ref, wcim_ref, d_ref,
               z_ref, hre_ref, him_ref, bu_ref, *, bn, tb):
    i = pl.program_id(0)
    half = are_ref.shape[1]
    n_slab = wb_ref.shape[0]
    kw = wb_ref.shape[1]
    sw = half // n_slab
    cw = 1024

    @pl.when(i == 0)
    def _():
        hre_ref[...] = h0re_ref[...]
        him_ref[...] = h0im_ref[...]

    ub = u_ref[...].astype(bf16)
    for s in range(n_slab):
        res = jnp.dot(ub[:, s * kw:(s + 1) * kw], wb_ref[s], preferred_element_type=f32)
        bu_ref[:, s * sw:(s + 1) * sw] = res[:, :sw]
        bu_ref[:, half + s * sw:half + (s + 1) * sw] = res[:, sw:]

    def advance(t, first):
        def sub(j, carry):
            rr = _aligned(t * bn + j * SUBLANES)
            for c0 in range(0, half, cw):
                if first:
                    jr = _aligned(j * SUBLANES)
                    p_re = hre_ref[pl.ds(jr, SUBLANES), c0:c0 + cw]
                    p_im = him_ref[pl.ds(jr, SUBLANES), c0:c0 + cw]
                else:
                    pr = _aligned(rr - bn)
                    p_re = bu_ref[pl.ds(pr, SUBLANES), c0:c0 + cw]
                    p_im = bu_ref[pl.ds(pr, SUBLANES), half + c0:half + c0 + cw]
                a_re = are_ref[:, c0:c0 + cw]
                a_im = aim_ref[:, c0:c0 + cw]
                n_re = a_re * p_re - a_im * p_im + bu_ref[pl.ds(rr, SUBLANES), c0:c0 + cw]
                n_im = a_re * p_im + a_im * p_re + bu_ref[pl.ds(rr, SUBLANES), half + c0:half + c0 + cw]
                bu_ref[pl.ds(rr, SUBLANES), c0:c0 + cw] = n_re
                bu_ref[pl.ds(rr, SUBLANES), half + c0:half + c0 + cw] = n_im
            return carry
        if bn == SUBLANES:
            sub(0, 0)
        else:
            lax.fori_loop(0, bn // SUBLANES, sub, 0)

    advance(0, True)
    if tb > 1:
        def body(t, carry):
            advance(t, False)
            return carry
        lax.fori_loop(1, tb, body, 0)
    hre_ref[...] = bu_ref[(tb - 1) * bn:tb * bn, 0:half]
    him_ref[...] = bu_ref[(tb - 1) * bn:tb * bn, half:2 * half]

    nw = wcre_ref.shape[2]
    for n in range(wcre_ref.shape[0]):
        h_re = bu_ref[:, n * sw:(n + 1) * sw]
        h_im = bu_ref[:, half + n * sw:half + (n + 1) * sw]
        y = _bdot(h_re, wcre_ref[n]) - _bdot(h_im, wcim_ref[n])
        y = y + d_ref[:, n * nw:(n + 1) * nw] * u_ref[:, n * nw:(n + 1) * nw]
        z_ref[:, n * nw:(n + 1) * nw] = jax.nn.gelu(y, approximate=True).astype(z_ref.dtype)


def _s5_scan(x, row0, n_steps, bn, h0_re, h0_im, a_re8, a_im8, wb, wc_re, wc_im, d_skip):
    d = x.shape[1]
    tb = max(1, S5_ROWS // bn)
    rows = tb * bn
    assert n_steps % tb == 0 and row0 % rows == 0
    half = a_re8.shape[1]
    blk0 = row0 // rows
    const2 = lambda i: (0, 0)
    const3 = lambda i: (0, 0, 0)
    z, h_re, h_im = pl.pallas_call(
        functools.partial(_s5_kernel, bn=bn, tb=tb),
        grid=(n_steps // tb,),
        in_specs=[
            pl.BlockSpec((rows, d), lambda i: (i + blk0, 0)),
            pl.BlockSpec((bn, half), const2),
            pl.BlockSpec((bn, half), const2),
            pl.BlockSpec(a_re8.shape, const2),
            pl.BlockSpec(a_im8.shape, const2),
            pl.BlockSpec(wb.shape, const3),
            pl.BlockSpec(wc_re.shape, const3),
            pl.BlockSpec(wc_im.shape, const3),
            pl.BlockSpec((1, d), const2),
        ],
        out_specs=[
            pl.BlockSpec((rows, d), lambda i: (i, 0)),
            pl.BlockSpec((bn, half), const2),
            pl.BlockSpec((bn, half), const2),
        ],
        out_shape=(jax.ShapeDtypeStruct((n_steps * bn, d), bf16),
                   jax.ShapeDtypeStruct((bn, half), f32),
                   jax.ShapeDtypeStruct((bn, half), f32)),
        scratch_shapes=[pltpu.VMEM((rows, 2 * half), f32)],
        compiler_params=_cparams(1),
        name="s5_scan",
    )(x, h0_re, h0_im, a_re8, a_im8, wb, wc_re, wc_im, d_skip)
    return z, h_re, h_im


def _s5_out_kernel(zp_ref, zs_ref, x_ref, wo_ref, wg_ref, lng_ref, lnb_ref,
                   wrh_ref, wrl_ref, rb_ref, tri_ref,
                   x1_ref, ri_ref, rf_ref, cnt_out_ref, cnt_ref, *, alpha, n_ptiles):
    @pl.when(pl.program_id(0) == 0)
    def _():
        cnt_ref[...] = jnp.zeros(cnt_ref.shape, f32)

    z = jnp.where(pl.program_id(0) < n_ptiles, zp_ref[...], zs_ref[...])
    mixed = (jnp.dot(z, wo_ref[...], preferred_element_type=f32)
             * jax.nn.sigmoid(jnp.dot(z, wg_ref[...], preferred_element_type=f32)))
    x1 = _layer_norm(alpha * x_ref[...] + mixed, lng_ref[...], lnb_ref[...])
    x1_ref[...] = x1
    _route(x1, wrh_ref[...], wrl_ref[...], rb_ref[...], tri_ref[...], cnt_ref, ri_ref, rf_ref,
           cnt_out_ref)


def _rw_out_kernel(op_ref, os_ref, g_ref, x_ref, wo_ref, lng_ref, lnb_ref,
                   wrh_ref, wrl_ref, rb_ref, tri_ref,
                   x1_ref, ri_ref, rf_ref, cnt_out_ref, cnt_ref, *, alpha, n_ptiles):
    @pl.when(pl.program_id(0) == 0)
    def _():
        cnt_ref[...] = jnp.zeros(cnt_ref.shape, f32)

    o = jnp.where(pl.program_id(0) < n_ptiles, op_ref[...], os_ref[...])
    mixed_t = jnp.dot(wo_ref[...], (o * g_ref[...]).astype(bf16), preferred_element_type=f32)
    mixed = jnp.transpose(mixed_t)
    x1 = _layer_norm(alpha * x_ref[...] + mixed, lng_ref[...], lnb_ref[...])
    x1_ref[...] = x1
    _route(x1, wrh_ref[...], wrl_ref[...], rb_ref[...], tri_ref[...], cnt_ref, ri_ref, rf_ref,
           cnt_out_ref)


def _post_mixer(kernel_fn, name, mix_p, mix_s, tok_inputs, weights, lng, lnb, router, alpha,
                channel_major=False):
    t, d = tok_inputs[-1].shape
    tm = TOK_TILE
    tok_axis = 1 if channel_major else 0
    assert t % tm == 0 and mix_p.shape[tok_axis] % tm == 0 and mix_s.shape[tok_axis] % tm == 0
    n_ptiles = mix_p.shape[tok_axis] // tm
    wr_hi, wr_lo, rbias, tri = router
    tile = lambda i: (i, 0)
    const = lambda i: (0, 0)
    if channel_major:
        blk = (d, tm)
        at = lambda f: (lambda i: (0, f(i)))
    else:
        blk = (tm, d)
        at = lambda f: (lambda i: (f(i), 0))
    in_specs = ([pl.BlockSpec(blk, at(lambda i: jnp.minimum(i, n_ptiles - 1))),
                 pl.BlockSpec(blk, at(lambda i: jnp.maximum(i - n_ptiles, 0)))]
                + [pl.BlockSpec(blk, at(lambda i: i)) for _ in tok_inputs[:-1]]
                + [pl.BlockSpec((tm, d), tile)]
                + [pl.BlockSpec(w.shape, const) for w in weights]
                + [pl.BlockSpec((1, d), const), pl.BlockSpec((1, d), const),
                   pl.BlockSpec(wr_hi.shape, const), pl.BlockSpec(wr_lo.shape, const),
                   pl.BlockSpec(rbias.shape, const), pl.BlockSpec(tri.shape, const)])
    x1, ri, rf, cnt = pl.pallas_call(
        functools.partial(kernel_fn, alpha=alpha, n_ptiles=n_ptiles),
        grid=(t // tm,),
        in_specs=in_specs,
        out_specs=[pl.BlockSpec((tm, d), tile),
                   pl.BlockSpec((SUBLANES, tm), lambda i: (0, i)),
                   pl.BlockSpec((SUBLANES, tm), lambda i: (0, i)),
                   pl.BlockSpec((N_EXPERTS, LANES), const)],
        out_shape=(jax.ShapeDtypeStruct((t, d), f32),
                   jax.ShapeDtypeStruct((SUBLANES, t), i32),
                   jax.ShapeDtypeStruct((SUBLANES, t), f32),
                   jax.ShapeDtypeStruct((N_EXPERTS, LANES), f32)),
        scratch_shapes=[pltpu.VMEM((N_EXPERTS, LANES), f32)],
        compiler_params=_cparams(1),
        name=name,
    )(mix_p, mix_s, *tok_inputs, *weights, lng, lnb, wr_hi, wr_lo, rbias, tri)
    return x1, ri, rf, cnt


def _dispatch_kernel(d1_ref, d2_ref, ends_ref, nu_ref, x_ref, xs_hbm, zero_buf, sem, zsem, *,
                     tile, ztile, n_ztiles):
    base = pl.program_id(0) * tile

    @pl.when(pl.program_id(0) == 0)
    def _():
        zero_buf[...] = jnp.zeros(zero_buf.shape, zero_buf.dtype)

        def zero_tile(idx):
            return pltpu.make_async_copy(zero_buf, xs_hbm.at[pl.ds(idx * ztile, ztile)], zsem)

        def each(fn):
            for e in range(N_EXPERTS):
                begin = ends_ref[e - 1] if e else 0
                pl.when(ends_ref[e] > begin)(lambda e=e: fn(ends_ref[e] // ztile - 1))

            def tail(idx, carry):
                pl.when(idx >= nu_ref[0])(lambda: fn(idx))
                return carry
            lax.fori_loop(0, n_ztiles, tail, 0)
        each(lambda idx: zero_tile(idx).start())
        each(lambda idx: zero_tile(idx).wait())

    def row_copy(j, dst):
        return pltpu.make_async_copy(x_ref.at[pl.ds(j, 1)], xs_hbm.at[pl.ds(dst, 1)], sem)

    def start(jq, carry):
        for u in range(DMA_UNROLL):
            j = jq * DMA_UNROLL + u
            row_copy(j, d1_ref[base + j]).start()
            row_copy(j, d2_ref[base + j]).start()
        return carry
    lax.fori_loop(0, tile // DMA_UNROLL, start, 0)

    def wait(jq, carry):
        for _ in range(2 * DMA_UNROLL):
            row_copy(0, 0).wait()
        return carry
    lax.fori_loop(0, tile // DMA_UNROLL, wait, 0)


def _dispatch(x1, d1, d2, ends, n_used, n_rows):
    t, d = x1.shape
    tile = DISPATCH_TILE
    ztile = EXPERT_TILE
    assert t % tile == 0 and n_rows % ztile == 0
    return pl.pallas_call(
        functools.partial(_dispatch_kernel, tile=tile, ztile=ztile, n_ztiles=n_rows // ztile),
        grid_spec=pltpu.PrefetchScalarGridSpec(
            num_scalar_prefetch=4,
            grid=(t // tile,),
            in_specs=[pl.BlockSpec((tile, d), lambda i, *_: (i, 0))],
            out_specs=pl.BlockSpec(memory_space=pl.ANY),
            scratch_shapes=[pltpu.VMEM((ztile, d), x1.dtype),
                            pltpu.SemaphoreType.DMA(()), pltpu.SemaphoreType.DMA(())],
        ),
        out_shape=jax.ShapeDtypeStruct((n_rows, d), x1.dtype),
        compiler_params=_cparams(1, has_side_effects=True),
        name="moe_dispatch",
    )(d1, d2, ends, n_used, x1)


def _expert_kernel(te_ref, nu_ref, xs_ref, wg_ref, wu_ref, wd_ref, ys_ref, wgb, wub, wdb):
    i = pl.program_id(0)
    used = i < nu_ref[0]
    prev = te_ref[jnp.maximum(i - 1, 0)]
    fresh = jnp.logical_or(i == 0, te_ref[i] != prev)

    @pl.when(jnp.logical_and(used, fresh))
    def _():
        wgb[...] = wg_ref[0].astype(bf16)
        wub[...] = wu_ref[0].astype(bf16)
        wdb[...] = wd_ref[0].astype(bf16)

    @pl.when(used)
    def _():
        x = xs_ref[...].astype(bf16)
        h = (jax.nn.silu(jnp.dot(x, wgb[...], preferred_element_type=f32))
             * jnp.dot(x, wub[...], preferred_element_type=f32))
        ys_ref[...] = jnp.dot(h.astype(bf16), wdb[...], preferred_element_type=f32)

    @pl.when(jnp.logical_not(used))
    def _():
        ys_ref[...] = jnp.zeros(ys_ref.shape, ys_ref.dtype)


def _experts(xs, tile_expert, n_used, wg, wu, wd):
    n_rows, d = xs.shape
    tile = EXPERT_TILE
    n_tiles = n_rows // tile
    dff = wg.shape[2]

    def row_map(i, te, nu):
        return (jnp.minimum(i, nu[0] - 1), 0)

    def w_map(i, te, nu):
        return (te[i], 0, 0)

    return pl.pallas_call(
        _expert_kernel,
        grid_spec=pltpu.PrefetchScalarGridSpec(
            num_scalar_prefetch=2,
            grid=(n_tiles,),
            in_specs=[pl.BlockSpec((tile, d), row_map),
                      pl.BlockSpec((1, d, dff), w_map),
                      pl.BlockSpec((1, d, dff), w_map),
                      pl.BlockSpec((1, dff, d), w_map)],
            out_specs=pl.BlockSpec((tile, d), lambda i, te, nu: (i, 0)),
            scratch_shapes=[pltpu.VMEM((d, dff), bf16), pltpu.VMEM((d, dff), bf16),
                            pltpu.VMEM((dff, d), bf16)],
        ),
        out_shape=jax.ShapeDtypeStruct((n_rows, d), f32),
        compiler_params=_cparams(1),
        name="moe_experts",
    )(tile_expert, n_used, xs, wg, wu, wd)


def _combine_kernel(d1_ref, d2_ref, ys_hbm, x1_ref, gate_ref, lng_ref, lnb_ref, out_ref,
                    y1_buf, y2_buf, sems, *, tile, alpha):
    i = pl.program_id(0)
    slot = i % 2

    def row_copy(src, buf, slot_, j):
        return pltpu.make_async_copy(ys_hbm.at[pl.ds(src, 1)], buf.at[slot_, pl.ds(j, 1)],
                                     sems.at[slot_])

    def gather(tile_idx, slot_):
        base = tile_idx * tile

        def start(jq, carry):
            for u in range(DMA_UNROLL):
                j = jq * DMA_UNROLL + u
                row_copy(d1_ref[base + j], y1_buf, slot_, j).start()
                row_copy(d2_ref[base + j], y2_buf, slot_, j).start()
            return carry
        lax.fori_loop(0, tile // DMA_UNROLL, start, 0)

    pl.when(i == 0)(lambda: gather(0, 0))
    pl.when(i + 1 < pl.num_programs(0))(lambda: gather(i + 1, 1 - slot))

    def wait(jq, carry):
        for _ in range(DMA_UNROLL):
            row_copy(0, y1_buf, slot, 0).wait()
            row_copy(0, y2_buf, slot, 0).wait()
        return carry
    lax.fori_loop(0, tile // DMA_UNROLL, wait, 0)

    gates = gate_ref[...]
    moe = gates[:, 0:1] * y1_buf[slot] + gates[:, 1:2] * y2_buf[slot]
    out_ref[...] = _layer_norm(alpha * x1_ref[...] + moe, lng_ref[...], lnb_ref[...])


def _combine(ys, x1, d1, d2, gates, lng, lnb, alpha):
    t, d = x1.shape
    tile = COMBINE_TILE
    assert t % tile == 0
    tok = lambda i, a, b: (i, 0)
    const = lambda i, a, b: (0, 0)
    return pl.pallas_call(
        functools.partial(_combine_kernel, tile=tile, alpha=alpha),
        grid_spec=pltpu.PrefetchScalarGridSpec(
            num_scalar_prefetch=2,
            grid=(t // tile,),
            in_specs=[pl.BlockSpec(memory_space=pl.ANY),
                      pl.BlockSpec((tile, d), tok),
                      pl.BlockSpec((tile, 2), tok),
                      pl.BlockSpec((1, d), const),
                      pl.BlockSpec((1, d), const)],
            out_specs=pl.BlockSpec((tile, d), tok),
            scratch_shapes=[pltpu.VMEM((2, tile, d), f32), pltpu.VMEM((2, tile, d), f32),
                            pltpu.SemaphoreType.DMA((2,))],
        ),
        out_shape=jax.ShapeDtypeStruct((t, d), f32),
        compiler_params=_cparams(1),
        name="moe_combine",
    )(d1, d2, ys, x1, gates, lng, lnb)


def _moe(x1, ri, rf, cnt, wg, wu, wd, lng, lnb, alpha):
    t, d = x1.shape
    tile = EXPERT_TILE
    n_tiles = -(-2 * t // tile) + N_EXPERTS
    counts = cnt[:, 0].astype(i32)
    padded = ((counts + tile - 1) // tile) * tile
    ends = jnp.cumsum(padded)
    starts = ends - padded
    n_used = (ends[-1] // tile).astype(i32).reshape(1)
    d1 = starts[ri[0]] + ri[2]
    d2 = starts[ri[1]] + ri[3]
    tile_start = jnp.minimum(jnp.arange(n_tiles, dtype=i32), n_used[0] - 1) * tile
    tile_expert = jnp.sum(tile_start[:, None] >= ends[None, :], axis=1).astype(i32)
    xs = _dispatch(x1, d1, d2, ends.astype(i32), n_used, n_tiles * tile)
    ys = _experts(xs, tile_expert, n_used, wg, wu, wd)
    gates = jnp.transpose(rf[0:2])
    return _combine(ys, x1, d1, d2, gates, lng, lnb, alpha)


def _rw_proj_kernel(*refs, has_vres, n_ptiles):
    if has_vres:
        (x_ref, hp_ref, hs_ref, sh_ref, vf_ref, mix_ref, wr_ref, wk_ref, wv_ref, w0_ref, w1_ref,
         w2_ref, a0_ref, a1_ref, a2_ref, g1_ref, g2_ref, kk_ref, ka_ref, v0_ref, v1_ref, v2_ref,
         r_out, w_out, k_out, v_out, kk_out, a_out, g_out, xp_buf) = refs
    else:
        (x_ref, hp_ref, hs_ref, sh_ref, mix_ref, wr_ref, wk_ref, wv_ref, w0_ref, w1_ref, w2_ref,
         a0_ref, a1_ref, a2_ref, g1_ref, g2_ref, kk_ref, ka_ref,
         r_out, w_out, k_out, v_out, kk_out, a_out, g_out, xp_buf) = refs
    i = pl.program_id(0)
    tm = x_ref.shape[0]
    bp, bs = hp_ref.shape[0], hs_ref.shape[0]

    @pl.when(i < n_ptiles)
    def _():
        xp_buf[0:bp, :] = jnp.where(i == 0, 0.0, hp_ref[...])
        if tm > bp:
            xp_buf[bp:tm, :] = x_ref[0:tm - bp, :]

    @pl.when(i >= n_ptiles)
    def _():
        xp_buf[0:bs, :] = jnp.where(i == n_ptiles, sh_ref[...], hs_ref[...])
        if tm > bs:
            xp_buf[bs:tm, :] = x_ref[0:tm - bs, :]

    x = x_ref[...]
    xx = xp_buf[...] - x

    def mixed(i):
        return (x + xx * mix_ref[i:i + 1, :]).astype(bf16)

    def proj(wt_ref, xm):
        return lax.dot_general(wt_ref[...], xm, (((1,), (1,)), ((), ())),
                               preferred_element_type=f32)

    def wdot(wt_ref, y):
        return jnp.dot(wt_ref[...], y.astype(bf16), preferred_element_type=f32)

    def col(p_ref):
        return jnp.tile(p_ref[...], (1, tm // LANES))

    r_out[...] = proj(wr_ref, mixed(0))
    lora_w = jnp.tanh(proj(w1_ref, mixed(1)))
    w = -_softplus(-(col(w0_ref) + wdot(w2_ref, lora_w))) - 0.5
    w_out[...] = jnp.exp(-jnp.exp(w))
    k = proj(wk_ref, mixed(2))
    xv = mixed(3)
    v = proj(wv_ref, xv)
    a = jax.nn.sigmoid(col(a0_ref) + wdot(a2_ref, proj(a1_ref, mixed(4))))
    a_out[...] = a
    gate = jax.nn.sigmoid(proj(g1_ref, mixed(5)))
    g_out[...] = wdot(g2_ref, gate)
    kk_out[...] = k * col(kk_ref)
    k_out[...] = k * (1.0 + (a - 1.0) * col(ka_ref))
    if has_vres:
        lora_v = wdot(v2_ref, proj(v1_ref, xv))
        v = v + (vf_ref[...] - v) * jax.nn.sigmoid(col(v0_ref) + lora_v)
    v_out[...] = v


def _rw_proj(x, tp, bp, bs, shift_s, v_first, weights):
    t, d = x.shape
    tm = PROJ_TILE
    assert t % tm == 0 and tp % tm == 0 and tm % bp == 0 and tm % bs == 0
    n_ptiles = tp // tm
    has_vres = v_first is not None
    tile = lambda i: (i, 0)
    const = lambda i: (0, 0)
    halo_p = lambda i: (jnp.maximum(jnp.minimum(i, n_ptiles - 1) * (tm // bp) - 1, 0), 0)
    halo_s = lambda i: (jnp.maximum(i, n_ptiles) * (tm // bs) - 1, 0)
    cols = lambda i: (0, i)
    tok = [x, x, x, shift_s] + ([v_first] if has_vres else [])
    tok_specs = [pl.BlockSpec((tm, d), tile), pl.BlockSpec((bp, d), halo_p),
                 pl.BlockSpec((bs, d), halo_s), pl.BlockSpec((bs, d), const)]
    if has_vres:
        tok_specs.append(pl.BlockSpec((d, tm), cols))
    outs = pl.pallas_call(
        functools.partial(_rw_proj_kernel, has_vres=has_vres, n_ptiles=n_ptiles),
        grid=(t // tm,),
        in_specs=tok_specs + [pl.BlockSpec(w.shape, const) for w in weights],
        out_specs=[pl.BlockSpec((d, tm), cols) for _ in range(7)],
        out_shape=tuple(jax.ShapeDtypeStruct((d, t), f32) for _ in range(7)),
        scratch_shapes=[pltpu.VMEM((tm, d), f32)],
        compiler_params=_cparams(1),
        name="rwkv_proj",
    )(*tok, *weights)
    return outs


def _block_transpose(src, dst, tmp):
    nb = src.shape[0]
    bw = LANES // nb
    blk = lax.broadcasted_iota(i32, tuple(src.shape[1:]), 1) // bw
    for i in range(nb):
        tmp[i] = pltpu.roll(src[i], i * bw, 1) if i else src[i]
    a, b = tmp, dst
    bit = 1
    while bit < nb:
        take = (blk & bit) != 0
        for i in range(nb):
            b[i] = jnp.where(take, a[(i + bit) % nb], a[i])
        a, b = b, a
        bit *= 2
    assert a is tmp
    for i in range(nb):
        dst[(nb - i) % nb] = pltpu.roll(tmp[i], i * bw, 1) if i else tmp[i]


def _wkv_kernel(r_ref, w_ref, k_ref, v_ref, kk_ref, a_ref, s0_ref, lng_ref, lnb_ref, rk_ref,
                o_ref, s_ref, cur, o_buf, tmp_buf, *, steps, time_on_lanes):
    n = s_ref.shape[0]
    R, W, K, V, KK, A = range(6)
    srcs = (r_ref, w_ref, k_ref, v_ref, kk_ref, a_ref)

    @pl.when(pl.program_id(1) == 0)
    def _():
        s_ref[...] = s0_ref[...]

    zeros = jnp.zeros((n, LANES), f32)

    def bcast(q, t, kidx):
        row = cur[q, t, pl.ds(kidx, 1), :]
        return jnp.broadcast_to(row, (n, LANES))

    def prep(t, carry):
        kk_raw = cur[KK, t]
        norm = jnp.sqrt(jnp.sum(kk_raw * kk_raw, axis=0, keepdims=True))
        kk = kk_raw / jnp.maximum(norm, 1e-12)
        cur[KK, t] = -kk
        cur[A, t] = kk * cur[A, t]
        return carry

    def state_dot(t):
        def body(kq, sa):
            for u in range(SUBLANES):
                kidx = kq * SUBLANES + u
                sa = sa + s_ref[kidx] * bcast(KK, t, kidx)
            return sa
        return lax.fori_loop(0, n // SUBLANES, body, zeros)

    def advance(t, sa, look_ahead):
        vv = cur[V, t]

        def body(kq, carry):
            out, sa_next = carry
            for u in range(SUBLANES):
                kidx = kq * SUBLANES + u
                s_new = (s_ref[kidx] * bcast(W, t, kidx) + sa * bcast(A, t, kidx)
                         + vv * bcast(K, t, kidx))
                s_ref[kidx] = s_new
                out = out + s_new * bcast(R, t, kidx)
                if look_ahead:
                    sa_next = sa_next + s_new * bcast(KK, t + 1, kidx)
            return out, sa_next
        out, sa_next = lax.fori_loop(0, n // SUBLANES, body, (zeros, zeros))

        mu = jnp.mean(out, axis=0, keepdims=True)
        dlt = out - mu
        var = jnp.mean(dlt * dlt, axis=0, keepdims=True)
        o_n = dlt * lax.rsqrt(var + GN_EPS) * lng_ref[...] + lnb_ref[...]
        bonus = jnp.sum(cur[R, t] * cur[K, t] * rk_ref[...], axis=0, keepdims=True) * vv
        o_buf[t] = o_n + bonus
        return sa_next

    for q, ref in enumerate(srcs):
        if time_on_lanes:
            _block_transpose(ref, cur.at[q], tmp_buf)
        else:
            for t in range(steps):
                cur[q, t] = ref[0, :, t * LANES:(t + 1) * LANES]

    lax.fori_loop(0, steps, prep, 0)
    sa = state_dot(0)
    if steps > 1:
        sa = lax.fori_loop(0, steps - 1, lambda t, s: advance(t, s, True), sa)
    advance(steps - 1, sa, False)

    if time_on_lanes:
        _block_transpose(o_buf, o_ref, tmp_buf)
    else:
        for t in range(steps):
            o_ref[0, :, t * LANES:(t + 1) * LANES] = o_buf[t]


def _wkv_scan(streams, col0, n_steps, bn, s0, lng, lnb, rk):
    d = streams[0].shape[0]
    n = HEAD_DIM
    heads = d // n
    time_on_lanes = bn * heads == LANES
    if time_on_lanes:
        steps = LANES // bn
        assert n_steps % steps == 0 and col0 % LANES == 0
        grid = (1, n_steps // steps)
        blk = (heads, n, LANES)
        c0 = col0 // LANES
        in_map = lambda j, t: (0, 0, c0 + t)
        out_map = lambda j, t: (0, 0, t)
    else:
        steps = n_steps
        assert bn == LANES and col0 % (steps * LANES) == 0
        grid = (heads, 1)
        blk = (1, n, steps * LANES)
        c0 = col0 // (steps * LANES)
        in_map = lambda j, t: (j, 0, c0)
        out_map = lambda j, t: (j, 0, 0)
    per_lane = lambda j, t: (0, 0, j)
    par = lambda j, t: (0, j)
    views = [s.reshape(heads, n, s.shape[1]) for s in streams]
    o, s_fin = pl.pallas_call(
        functools.partial(_wkv_kernel, steps=steps, time_on_lanes=time_on_lanes),
        grid=grid,
        in_specs=([pl.BlockSpec(blk, in_map) for _ in range(6)]
                  + [pl.BlockSpec((n, n, LANES), per_lane)]
                  + [pl.BlockSpec((n, LANES), par) for _ in range(3)]),
        out_specs=[pl.BlockSpec(blk, out_map),
                   pl.BlockSpec((n, n, LANES), per_lane)],
        out_shape=(jax.ShapeDtypeStruct((heads, n, n_steps * bn), f32),
                   jax.ShapeDtypeStruct((n, n, heads * bn), f32)),
        scratch_shapes=[pltpu.VMEM((6, steps, n, LANES), f32), pltpu.VMEM((steps, n, LANES), f32),
                        pltpu.VMEM((steps, n, LANES), f32)],
        compiler_params=_cparams(2),
        name="wkv_scan",
    )(*views, s0, lng, lnb, rk)
    return o.reshape(d, n_steps * bn), s_fin


def _reorder_plan(shape_p, shape_s):
    bp, lp, d = shape_p
    bs, ls, _ = shape_s
    rows = bs * ls
    assert rows % bp == 0 and (bp * lp) % rows == 0
    return bp, bs, ls, d, rows, rows // bp, (bp * lp) // rows


def _to_time_major_kernel(xp_ref, xs_ref, o_ref, *, n_pblocks):
    bp, tb, _ = xp_ref.shape
    bs, ls, _ = xs_ref.shape

    @pl.when(pl.program_id(0) < n_pblocks)
    def _():
        for t in range(tb):
            o_ref[t * bp:(t + 1) * bp, :] = xp_ref[:, t, :]

    @pl.when(pl.program_id(0) >= n_pblocks)
    def _():
        for t in range(ls):
            o_ref[t * bs:(t + 1) * bs, :] = xs_ref[:, t, :]


def _to_time_major(x_prompt, x_sample):
    bp, bs, ls, d, rows, tb, n_pblocks = _reorder_plan(x_prompt.shape, x_sample.shape)
    return pl.pallas_call(
        functools.partial(_to_time_major_kernel, n_pblocks=n_pblocks),
        grid=(n_pblocks + 1,),
        in_specs=[pl.BlockSpec((bp, tb, d), lambda i: (0, jnp.minimum(i, n_pblocks - 1), 0)),
                  pl.BlockSpec((bs, ls, d), lambda i: (0, 0, 0))],
        out_specs=pl.BlockSpec((rows, d), lambda i: (i, 0)),
        out_shape=jax.ShapeDtypeStruct(((n_pblocks + 1) * rows, d), x_prompt.dtype),
        compiler_params=_cparams(1),
        name="to_time_major",
    )(x_prompt, x_sample)


def _to_batch_major_kernel(x_ref, yp_ref, ys_ref, *, n_pblocks):
    bp, tb, _ = yp_ref.shape
    bs, ls, _ = ys_ref.shape

    @pl.when(pl.program_id(0) < n_pblocks)
    def _():
        for t in range(tb):
            yp_ref[:, t, :] = x_ref[t * bp:(t + 1) * bp, :]

    @pl.when(pl.program_id(0) >= n_pblocks)
    def _():
        for t in range(ls):
            ys_ref[:, t, :] = x_ref[t * bs:(t + 1) * bs, :]


def _to_batch_major(x, shape_p, shape_s):
    bp, bs, ls, d, rows, tb, n_pblocks = _reorder_plan(shape_p, shape_s)
    return pl.pallas_call(
        functools.partial(_to_batch_major_kernel, n_pblocks=n_pblocks),
        grid=(n_pblocks + 1,),
        in_specs=[pl.BlockSpec((rows, d), lambda i: (i, 0))],
        out_specs=[pl.BlockSpec((bp, tb, d), lambda i: (0, jnp.minimum(i, n_pblocks - 1), 0)),
                   pl.BlockSpec((bs, ls, d), lambda i: (0, 0, 0))],
        out_shape=(jax.ShapeDtypeStruct(shape_p, x.dtype), jax.ShapeDtypeStruct(shape_s, x.dtype)),
        compiler_params=_cparams(1),
        name="to_batch_major",
    )(x)


def _lanes_param(p, bn):
    h = p.size // HEAD_DIM
    y = jnp.transpose(p.reshape(h, HEAD_DIM))
    return jnp.repeat(y, bn, axis=1)


def _col_param(p):
    return jnp.broadcast_to(p.reshape(-1, 1), (p.size, LANES))


def _pad_cols(w, mult=LANES):
    pad = (-w.shape[1]) % mult
    return jnp.pad(w, ((0, 0), (0, pad)))


def _pad_rows(w, mult=LANES):
    pad = (-w.shape[0]) % mult
    return jnp.pad(w, ((0, pad), (0, 0)))


def kernel(x_prompt, x_sample, state_s5_re, state_s5_im, state_wkv, state_shift, ln_gain, ln_bias,
           s5_lambda_re, s5_lambda_im, s5_log_step, s5_b_re, s5_b_im, s5_c_re, s5_c_im, s5_d,
           s5_w_out, s5_w_gate, rw_mix, rw_w_r, rw_w_k, rw_w_v, rw_w_o, rw_w0, rw_w1, rw_w2,
           rw_a0, rw_a1, rw_a2, rw_v0, rw_v1, rw_v2, rw_g1, rw_g2, rw_k_k, rw_k_a, rw_r_k,
           rw_lnx_g, rw_lnx_b, w_router, router_bias, moe_w_gate, moe_w_up, moe_w_down):
    bp, lp, d = x_prompt.shape
    bs, ls, _ = x_sample.shape
    tp, ts = bp * lp, bs * ls
    depth = ln_gain.shape[0]
    alpha = (2.0 * depth) ** 0.25
    n_groups = s5_lambda_re.shape[1]
    n_state = s5_lambda_re.shape[2]
    half = n_groups * n_state
    heads = d // HEAD_DIM

    x = _to_time_major(x_prompt, x_sample)

    wr_t = jnp.transpose(w_router)
    wr_hi = wr_t.astype(bf16)
    wr_lo = (wr_t - wr_hi.astype(f32)).astype(bf16)
    rbias = router_bias.reshape(N_EXPERTS, 1)
    tri = jnp.triu(jnp.ones((TOK_TILE, TOK_TILE), bf16), k=1)
    router = (wr_hi, wr_lo, rbias, tri)

    new_re_p, new_im_p, new_re_s, new_im_s = [], [], [], []
    new_wkv_p, new_wkv_s, new_shift_p, new_shift_s = [], [], [], []
    v_first = None
    groups_per_slab = MXU_DIM // S5_GROUP

    for i in range(depth):
        j = i // 2
        if i % 2 == 0:
            a_re, a_im, bb_re, bb_im = _s5_discretise(s5_lambda_re[j], s5_lambda_im[j], s5_log_step[j],
                                                      s5_b_re[j], s5_b_im[j])
            a_re8 = jnp.broadcast_to(a_re.reshape(1, half), (SUBLANES, half))
            a_im8 = jnp.broadcast_to(a_im.reshape(1, half), (SUBLANES, half))
            n_slab = n_groups // groups_per_slab

            def pack_b(bb):
                w = jnp.transpose(bb, (1, 0, 2)).reshape(n_slab, groups_per_slab, S5_GROUP, n_state)
                return _block_diag(w, groups_per_slab)
            wb = jnp.concatenate([pack_b(bb_re), pack_b(bb_im)], axis=2).astype(bf16)

            def pack_c(c):
                w = jnp.transpose(c, (0, 2, 1)).reshape(n_slab, groups_per_slab, n_state, S5_GROUP)
                return _block_diag(w, groups_per_slab).astype(bf16)
            wc_re, wc_im = pack_c(s5_c_re[j]), pack_c(s5_c_im[j])
            d_skip = s5_d[j].reshape(1, d)

            zero_h = jnp.zeros((bp, half), f32)
            z_p, hre_p, him_p = _s5_scan(x, 0, lp, bp, zero_h, zero_h, a_re8, a_im8,
                                         wb, wc_re, wc_im, d_skip)
            z_s, hre_s, him_s = _s5_scan(x, tp, ls, bs, state_s5_re[j].reshape(bs, half),
                                         state_s5_im[j].reshape(bs, half), a_re8, a_im8,
                                         wb, wc_re, wc_im, d_skip)
            new_re_p.append(hre_p.reshape(bp, n_groups, n_state))
            new_im_p.append(him_p.reshape(bp, n_groups, n_state))
            new_re_s.append(hre_s.reshape(bs, n_groups, n_state))
            new_im_s.append(him_s.reshape(bs, n_groups, n_state))
            x1, ri, rf, cnt = _post_mixer(
                _s5_out_kernel, "s5_out", z_p, z_s, [x],
                [s5_w_out[j].astype(bf16), s5_w_gate[j].astype(bf16)],
                ln_gain[i, 0].reshape(1, d), ln_bias[i, 0].reshape(1, d), router, alpha)
        else:
            new_shift_p.append(x[tp - bp:tp])
            new_shift_s.append(x[tp + ts - bs:tp + ts])
            wt = lambda w: jnp.transpose(w).astype(bf16)
            weights = [rw_mix[j],
                       wt(rw_w_r[j]), wt(rw_w_k[j]), wt(rw_w_v[j]),
                       _col_param(rw_w0[j]), wt(_pad_cols(rw_w1[j])), wt(_pad_rows(rw_w2[j])),
                       _col_param(rw_a0[j]), wt(_pad_cols(rw_a1[j])), wt(_pad_rows(rw_a2[j])),
                       wt(_pad_cols(rw_g1[j], MXU_DIM)), wt(_pad_rows(rw_g2[j], MXU_DIM)),
                       _col_param(rw_k_k[j]), _col_param(rw_k_a[j])]
            if j > 0:
                weights += [_col_param(rw_v0[j - 1]), wt(_pad_cols(rw_v1[j - 1])),
                            wt(_pad_rows(rw_v2[j - 1]))]
            r, w, k, v, kk, a, g = _rw_proj(x, tp, bp, bs, state_shift[j],
                                            v_first if j > 0 else None, weights)
            if j == 0:
                v_first = v

            def run_group(col0, n_steps, bn, s0):
                o, s_fin = _wkv_scan((r, w, k, v, kk, a), col0, n_steps, bn, s0,
                                     _lanes_param(rw_lnx_g[j], bn), _lanes_param(rw_lnx_b[j], bn),
                                     _lanes_param(rw_r_k[j], bn))
                s_fin = jnp.transpose(s_fin.reshape(HEAD_DIM, HEAD_DIM, heads, bn), (3, 2, 1, 0))
                return o, s_fin

            o_p, s_p = run_group(0, lp, bp, jnp.zeros((HEAD_DIM, HEAD_DIM, heads * bp), f32))
            s0_s = jnp.transpose(state_wkv[j], (3, 2, 1, 0)).reshape(HEAD_DIM, HEAD_DIM, heads * bs)
            o_s, s_s = run_group(tp, ls, bs, s0_s)
            new_wkv_p.append(s_p)
            new_wkv_s.append(s_s)
            x1, ri, rf, cnt = _post_mixer(
                _rw_out_kernel, "rwkv_out", o_p, o_s, [g, x], [wt(rw_w_o[j])],
                ln_gain[i, 0].reshape(1, d), ln_bias[i, 0].reshape(1, d), router, alpha,
                channel_major=True)

        x = _moe(x1, ri, rf, cnt, moe_w_gate[i], moe_w_up[i], moe_w_down[i],
                 ln_gain[i, 1].reshape(1, d), ln_bias[i, 1].reshape(1, d), alpha)

    y_prompt, y_sample = _to_batch_major(x, x_prompt.shape, x_sample.shape)
    return (y_prompt, y_sample,
            jnp.stack(new_re_p), jnp.stack(new_im_p), jnp.stack(new_wkv_p), jnp.stack(new_shift_p),
            jnp.stack(new_re_s), jnp.stack(new_im_s), jnp.stack(new_wkv_s), jnp.stack(new_shift_s))
```

```python
import functools
import math

import jax
import jax.numpy as jnp
from jax import lax
from jax.experimental import pallas as pl
from jax.experimental.pallas import tpu as pltpu

f32 = jnp.float32
bf16 = jnp.bfloat16
i32 = jnp.int32

S5_GROUP = 16
S5_STATE = 64
HEAD_DIM = 64
GN_EPS = 64e-5
LN_EPS = 1e-5
N_EXPERTS = 16
N_EXPERT_GROUPS = 4
EXPERTS_PER_GROUP = N_EXPERTS // N_EXPERT_GROUPS

SUBLANES = 8
LANES = 128
MXU_DIM = 256
VMEM_LIMIT = 56 * 1024 * 1024

TOK_TILE = 512
PROJ_TILE = 256
S5_ROWS = 256
EXPERT_TILE = 256
COMBINE_TILE = 256
DISPATCH_TILE = 512
WKV_STEPS = 16
DMA_UNROLL = 8


def _cparams(n_axes=1, **kw):
    return pltpu.CompilerParams(dimension_semantics=("arbitrary",) * n_axes,
                                vmem_limit_bytes=VMEM_LIMIT, **kw)


def _bdot(a, w):
    return jnp.dot(a.astype(bf16), w, preferred_element_type=f32)


def _layer_norm(y, g, b):
    mu = jnp.mean(y, axis=-1, keepdims=True)
    d = y - mu
    var = jnp.mean(d * d, axis=-1, keepdims=True)
    return d * lax.rsqrt(var + LN_EPS) * g + b


def _aligned(idx):
    return idx if isinstance(idx, int) else pl.multiple_of(idx, SUBLANES)


def _softplus(x):
    return jnp.maximum(x, 0.0) + jnp.log1p(jnp.exp(-jnp.abs(x)))


def _route(x1, wr_hi, wr_lo, rbias, tri, cnt_ref, ri_ref, rf_ref, cnt_out_ref):
    tm = x1.shape[0]
    xh = x1.astype(bf16)
    xl = (x1 - xh.astype(f32)).astype(bf16)
    dn = (((1,), (1,)), ((), ()))
    logits = (lax.dot_general(wr_hi, xh, dn, preferred_element_type=f32)
              + lax.dot_general(wr_lo, xh, dn, preferred_element_type=f32)
              + lax.dot_general(wr_hi, xl, dn, preferred_element_type=f32))
    m = jnp.max(logits, axis=0, keepdims=True)
    ex = jnp.exp(logits - m)
    probs = ex / jnp.sum(ex, axis=0, keepdims=True)
    sel = probs + rbias
    s = [sel[e:e + 1, :] for e in range(N_EXPERTS)]
    p = [probs[e:e + 1, :] for e in range(N_EXPERTS)]

    best = None
    gi = None
    for g in range(N_EXPERT_GROUPS):
        a, b, c, d = s[4 * g:4 * g + 4]
        hi1, lo1 = jnp.maximum(a, b), jnp.minimum(a, b)
        hi2, lo2 = jnp.maximum(c, d), jnp.minimum(c, d)
        score = jnp.maximum(hi1, hi2) + jnp.maximum(jnp.minimum(hi1, hi2), jnp.maximum(lo1, lo2))
        if g == 0:
            best, gi = score, jnp.zeros(score.shape, i32)
        else:
            better = score > best
            best = jnp.where(better, score, best)
            gi = jnp.where(better, g, gi)

    def in_group(rows, j):
        out = rows[12 + j]
        for g in (2, 1, 0):
            out = jnp.where(gi == g, rows[4 * g + j], out)
        return out

    v = [in_group(s, j) for j in range(EXPERTS_PER_GROUP)]
    pv = [in_group(p, j) for j in range(EXPERTS_PER_GROUP)]
    order = []
    for j in range(EXPERTS_PER_GROUP):
        r = jnp.zeros(v[j].shape, i32)
        for i in range(EXPERTS_PER_GROUP):
            if i < j:
                r = r + jnp.where(v[i] >= v[j], 1, 0)
            elif i > j:
                r = r + jnp.where(v[i] > v[j], 1, 0)
        order.append(r)
    j1 = sum(jnp.where(order[j] == 0, j, 0) for j in range(EXPERTS_PER_GROUP))
    j2 = sum(jnp.where(order[j] == 1, j, 0) for j in range(EXPERTS_PER_GROUP))
    p1 = sum(jnp.where(order[j] == 0, pv[j], 0.0) for j in range(EXPERTS_PER_GROUP))
    p2 = sum(jnp.where(order[j] == 1, pv[j], 0.0) for j in range(EXPERTS_PER_GROUP))
    e1 = gi * EXPERTS_PER_GROUP + j1
    e2 = gi * EXPERTS_PER_GROUP + j2
    den = p1 + p2
    g1 = p1 / den
    g2 = p2 / den

    eio = lax.broadcasted_iota(i32, (N_EXPERTS, tm), 0)
    oh1 = eio == e1
    oh2 = eio == e2
    oh = jnp.where(oh1, 1.0, 0.0) + jnp.where(oh2, 1.0, 0.0)
    before = jnp.dot(oh.astype(bf16), tri, preferred_element_type=f32)
    before = before + cnt_ref[:, 0:1]
    r1 = jnp.sum(jnp.where(oh1, before, 0.0), axis=0, keepdims=True)
    r2 = jnp.sum(jnp.where(oh2, before, 0.0), axis=0, keepdims=True)
    cnt_new = cnt_ref[...] + jnp.sum(oh, axis=1, keepdims=True)
    cnt_ref[...] = cnt_new
    cnt_out_ref[...] = cnt_new

    ri_ref[...] = jnp.zeros(ri_ref.shape, i32)
    rf_ref[...] = jnp.zeros(rf_ref.shape, f32)
    ri_ref[0:1, :] = e1
    ri_ref[1:2, :] = e2
    ri_ref[2:3, :] = r1.astype(i32)
    ri_ref[3:4, :] = r2.astype(i32)
    rf_ref[0:1, :] = g1
    rf_ref[1:2, :] = g2


def _s5_disc_kernel(lre_ref, lim_ref, lstep_ref, bre_ref, bim_ref,
                    are_ref, aim_ref, bbre_ref, bbim_ref):
    lre = lre_ref[...]
    lim = lim_ref[...]
    step = jnp.exp(lstep_ref[...])
    mag = jnp.exp(lre * step)
    th = lim * step
    a_re = mag * jnp.cos(th)
    a_im = mag * jnp.sin(th)
    den = lre * lre + lim * lim
    q_re = ((a_re - 1.0) * lre + a_im * lim) / den
    q_im = (a_im * lre - (a_re - 1.0) * lim) / den
    are_ref[...] = a_re
    aim_ref[...] = a_im
    b_re = bre_ref[...]
    b_im = bim_ref[...]
    bbre_ref[...] = q_re[None] * b_re - q_im[None] * b_im
    bbim_ref[...] = q_re[None] * b_im + q_im[None] * b_re


def _s5_discretise(lam_re, lam_im, log_step, b_re, b_im):
    g, p = lam_re.shape
    c = b_re.shape[-1]
    outs = pl.pallas_call(
        _s5_disc_kernel,
        out_shape=(jax.ShapeDtypeStruct((g, p), f32), jax.ShapeDtypeStruct((g, p), f32),
                   jax.ShapeDtypeStruct((c, g, p), f32), jax.ShapeDtypeStruct((c, g, p), f32)),
        name="s5_discretise",
    )(lam_re, lam_im, log_step.reshape(g, 1),
      jnp.transpose(b_re, (2, 0, 1)), jnp.transpose(b_im, (2, 0, 1)))
    return outs


def _block_diag(w, n_blk):
    s, _, a, b = w.shape
    eye = jnp.eye(n_blk, dtype=w.dtype)
    out = w[:, :, :, None, :] * eye[None, :, None, :, None]
    return out.reshape(s, n_blk * a, n_blk * b)


def _s5_kernel(u_ref, h0re_ref, h0im_ref, are_ref, aim_ref, wb_ref, wcre_ref, wcim_ref, d_ref,
               z_ref, hre_ref, him_ref, bu_ref, *, bn, tb):
    i = pl.program_id(0)
    half = are_ref.shape[1]
    n_slab = wb_ref.shape[0]
    kw = wb_ref.shape[1]
    sw = half // n_slab
    cw = 1024

    @pl.when(i == 0)
    def _():
        hre_ref[...] = h0re_ref[...]
        him_ref[...] = h0im_ref[...]

    ub = u_ref[...].astype(bf16)
    for s in range(n_slab):
        res = jnp.dot(ub[:, s * kw:(s + 1) * kw], wb_ref[s], preferred_element_type=f32)
        bu_ref[:, s * sw:(s + 1) * sw] = res[:, :sw]
        bu_ref[:, half + s * sw:half + (s + 1) * sw] = res[:, sw:]

    def advance(t, first):
        def sub(j, carry):
            rr = _aligned(t * bn + j * SUBLANES)
            for c0 in range(0, half, cw):
                if first:
                    jr = _aligned(j * SUBLANES)
                    p_re = hre_ref[pl.ds(jr, SUBLANES), c0:c0 + cw]
                    p_im = him_ref[pl.ds(jr, SUBLANES), c0:c0 + cw]
                else:
                    pr = _aligned(rr - bn)
                    p_re = bu_ref[pl.ds(pr, SUBLANES), c0:c0 + cw]
                    p_im = bu_ref[pl.ds(pr, SUBLANES), half + c0:half + c0 + cw]
                a_re = are_ref[:, c0:c0 + cw]
                a_im = aim_ref[:, c0:c0 + cw]
                n_re = a_re * p_re - a_im * p_im + bu_ref[pl.ds(rr, SUBLANES), c0:c0 + cw]
                n_im = a_re * p_im + a_im * p_re + bu_ref[pl.ds(rr, SUBLANES), half + c0:half + c0 + cw]
                bu_ref[pl.ds(rr, SUBLANES), c0:c0 + cw] = n_re
                bu_ref[pl.ds(rr, SUBLANES), half + c0:half + c0 + cw] = n_im
            return carry
        if bn == SUBLANES:
            sub(0, 0)
        else:
            lax.fori_loop(0, bn // SUBLANES, sub, 0)

    advance(0, True)
    if tb > 1:
        def body(t, carry):
            advance(t, False)
            return carry
        lax.fori_loop(1, tb, body, 0)
    hre_ref[...] = bu_ref[(tb - 1) * bn:tb * bn, 0:half]
    him_ref[...] = bu_ref[(tb - 1) * bn:tb * bn, half:2 * half]

    nw = wcre_ref.shape[2]
    for n in range(wcre_ref.shape[0]):
        h_re = bu_ref[:, n * sw:(n + 1) * sw]
        h_im = bu_ref[:, half + n * sw:half + (n + 1) * sw]
        y = _bdot(h_re, wcre_ref[n]) - _bdot(h_im, wcim_ref[n])
        y = y + d_ref[:, n * nw:(n + 1) * nw] * u_ref[:, n * nw:(n + 1) * nw]
        z_ref[:, n * nw:(n + 1) * nw] = jax.nn.gelu(y, approximate=True).astype(z_ref.dtype)


def _s5_scan(x, row0, n_steps, bn, h0_re, h0_im, a_re8, a_im8, wb, wc_re, wc_im, d_skip):
    d = x.shape[1]
    tb = max(1, S5_ROWS // bn)
    rows = tb * bn
    assert n_steps % tb == 0 and row0 % rows == 0
    half = a_re8.shape[1]
    blk0 = row0 // rows
    const2 = lambda i: (0, 0)
    const3 = lambda i: (0, 0, 0)
    z, h_re, h_im = pl.pallas_call(
        functools.partial(_s5_kernel, bn=bn, tb=tb),
        grid=(n_steps // tb,),
        in_specs=[
            pl.BlockSpec((rows, d), lambda i: (i + blk0, 0)),
            pl.BlockSpec((bn, half), const2),
            pl.BlockSpec((bn, half), const2),
            pl.BlockSpec(a_re8.shape, const2),
            pl.BlockSpec(a_im8.shape, const2),
            pl.BlockSpec(wb.shape, const3),
            pl.BlockSpec(wc_re.shape, const3),
            pl.BlockSpec(wc_im.shape, const3),
            pl.BlockSpec((1, d), const2),
        ],
        out_specs=[
            pl.BlockSpec((rows, d), lambda i: (i, 0)),
            pl.BlockSpec((bn, half), const2),
            pl.BlockSpec((bn, half), const2),
        ],
        out_shape=(jax.ShapeDtypeStruct((n_steps * bn, d), bf16),
                   jax.ShapeDtypeStruct((bn, half), f32),
                   jax.ShapeDtypeStruct((bn, half), f32)),
        scratch_shapes=[pltpu.VMEM((rows, 2 * half), f32)],
        compiler_params=_cparams(1),
        name="s5_scan",
    )(x, h0_re, h0_im, a_re8, a_im8, wb, wc_re, wc_im, d_skip)
    return z, h_re, h_im


def _s5_out_kernel(zp_ref, zs_ref, x_ref, wo_ref, wg_ref, lng_ref, lnb_ref,
                   wrh_ref, wrl_ref, rb_ref, tri_ref,
                   x1_ref, ri_ref, rf_ref, cnt_out_ref, cnt_ref, *, alpha, n_ptiles):
    @pl.when(pl.program_id(0) == 0)
    def _():
        cnt_ref[...] = jnp.zeros(cnt_ref.shape, f32)

    z = jnp.where(pl.program_id(0) < n_ptiles, zp_ref[...], zs_ref[...])
    mixed = (jnp.dot(z, wo_ref[...], preferred_element_type=f32)
             * jax.nn.sigmoid(jnp.dot(z, wg_ref[...], preferred_element_type=f32)))
    x1 = _layer_norm(alpha * x_ref[...] + mixed, lng_ref[...], lnb_ref[...])
    x1_ref[...] = x1
    _route(x1, wrh_ref[...], wrl_ref[...], rb_ref[...], tri_ref[...], cnt_ref, ri_ref, rf_ref,
           cnt_out_ref)


def _rw_out_kernel(op_ref, os_ref, g_ref, x_ref, wo_ref, lng_ref, lnb_ref,
                   wrh_ref, wrl_ref, rb_ref, tri_ref,
                   x1_ref, ri_ref, rf_ref, cnt_out_ref, cnt_ref, *, alpha, n_ptiles):
    @pl.when(pl.program_id(0) == 0)
    def _():
        cnt_ref[...] = jnp.zeros(cnt_ref.shape, f32)

    o = jnp.where(pl.program_id(0) < n_ptiles, op_ref[...], os_ref[...])
    mixed_t = jnp.dot(wo_ref[...], (o * g_ref[...]).astype(bf16), preferred_element_type=f32)
    mixed = jnp.transpose(mixed_t)
    x1 = _layer_norm(alpha * x_ref[...] + mixed, lng_ref[...], lnb_ref[...])
    x1_ref[...] = x1
    _route(x1, wrh_ref[...], wrl_ref[...], rb_ref[...], tri_ref[...], cnt_ref, ri_ref, rf_ref,
           cnt_out_ref)


def _post_mixer(kernel_fn, name, mix_p, mix_s, tok_inputs, weights, lng, lnb, router, alpha,
                channel_major=False):
    t, d = tok_inputs[-1].shape
    tm = TOK_TILE
    tok_axis = 1 if channel_major else 0
    assert t % tm == 0 and mix_p.shape[tok_axis] % tm == 0 and mix_s.shape[tok_axis] % tm == 0
    n_ptiles = mix_p.shape[tok_axis] // tm
    wr_hi, wr_lo, rbias, tri = router
    tile = lambda i: (i, 0)
    const = lambda i: (0, 0)
    if channel_major:
        blk = (d, tm)
        at = lambda f: (lambda i: (0, f(i)))
    else:
        blk = (tm, d)
        at = lambda f: (lambda i: (f(i), 0))
    in_specs = ([pl.BlockSpec(blk, at(lambda i: jnp.minimum(i, n_ptiles - 1))),
                 pl.BlockSpec(blk, at(lambda i: jnp.maximum(i - n_ptiles, 0)))]
                + [pl.BlockSpec(blk, at(lambda i: i)) for _ in tok_inputs[:-1]]
                + [pl.BlockSpec((tm, d), tile)]
                + [pl.BlockSpec(w.shape, const) for w in weights]
                + [pl.BlockSpec((1, d), const), pl.BlockSpec((1, d), const),
                   pl.BlockSpec(wr_hi.shape, const), pl.BlockSpec(wr_lo.shape, const),
                   pl.BlockSpec(rbias.shape, const), pl.BlockSpec(tri.shape, const)])
    x1, ri, rf, cnt = pl.pallas_call(
        functools.partial(kernel_fn, alpha=alpha, n_ptiles=n_ptiles),
        grid=(t // tm,),
        in_specs=in_specs,
        out_specs=[pl.BlockSpec((tm, d), tile),
                   pl.BlockSpec((SUBLANES, tm), lambda i: (0, i)),
                   pl.BlockSpec((SUBLANES, tm), lambda i: (0, i)),
                   pl.BlockSpec((N_EXPERTS, LANES), const)],
        out_shape=(jax.ShapeDtypeStruct((t, d), f32),
                   jax.ShapeDtypeStruct((SUBLANES, t), i32),
                   jax.ShapeDtypeStruct((SUBLANES, t), f32),
                   jax.ShapeDtypeStruct((N_EXPERTS, LANES), f32)),
        scratch_shapes=[pltpu.VMEM((N_EXPERTS, LANES), f32)],
        compiler_params=_cparams(1),
        name=name,
    )(mix_p, mix_s, *tok_inputs, *weights, lng, lnb, wr_hi, wr_lo, rbias, tri)
    return x1, ri, rf, cnt


def _dispatch_kernel(d1_ref, d2_ref, ends_ref, nu_ref, x_ref, xs_hbm, zero_buf, sem, zsem, *,
                     tile, ztile, n_ztiles):
    base = pl.program_id(0) * tile

    @pl.when(pl.program_id(0) == 0)
    def _():
        zero_buf[...] = jnp.zeros(zero_buf.shape, zero_buf.dtype)

        def zero_tile(idx):
            return pltpu.make_async_copy(zero_buf, xs_hbm.at[pl.ds(idx * ztile, ztile)], zsem)

        def each(fn):
            for e in range(N_EXPERTS):
                begin = ends_ref[e - 1] if e else 0
                pl.when(ends_ref[e] > begin)(lambda e=e: fn(ends_ref[e] // ztile - 1))

            def tail(idx, carry):
                pl.when(idx >= nu_ref[0])(lambda: fn(idx))
                return carry
            lax.fori_loop(0, n_ztiles, tail, 0)
        each(lambda idx: zero_tile(idx).start())
        each(lambda idx: zero_tile(idx).wait())

    def row_copy(j, dst):
        return pltpu.make_async_copy(x_ref.at[pl.ds(j, 1)], xs_hbm.at[pl.ds(dst, 1)], sem)

    def start(jq, carry):
        for u in range(DMA_UNROLL):
            j = _aligned(jq * DMA_UNROLL) + u
            row_copy(j, d1_ref[base + j]).start()
            row_copy(j, d2_ref[base + j]).start()
        return carry
    lax.fori_loop(0, tile // DMA_UNROLL, start, 0)

    def wait(jq, carry):
        for _ in range(2 * DMA_UNROLL):
            row_copy(0, 0).wait()
        return carry
    lax.fori_loop(0, tile // DMA_UNROLL, wait, 0)


def _dispatch(x1, d1, d2, ends, n_used, n_rows):
    t, d = x1.shape
    tile = DISPATCH_TILE
    ztile = EXPERT_TILE
    assert t % tile == 0 and n_rows % ztile == 0
    return pl.pallas_call(
        functools.partial(_dispatch_kernel, tile=tile, ztile=ztile, n_ztiles=n_rows // ztile),
        grid_spec=pltpu.PrefetchScalarGridSpec(
            num_scalar_prefetch=4,
            grid=(t // tile,),
            in_specs=[pl.BlockSpec((tile, d), lambda i, *_: (i, 0))],
            out_specs=pl.BlockSpec(memory_space=pl.ANY),
            scratch_shapes=[pltpu.VMEM((ztile, d), x1.dtype),
                            pltpu.SemaphoreType.DMA(()), pltpu.SemaphoreType.DMA(())],
        ),
        out_shape=jax.ShapeDtypeStruct((n_rows, d), x1.dtype),
        compiler_params=_cparams(1, has_side_effects=True),
        name="moe_dispatch",
    )(d1, d2, ends, n_used, x1)


def _expert_kernel(te_ref, nu_ref, xs_ref, wg_ref, wu_ref, wd_ref, ys_ref, wgb, wub, wdb):
    i = pl.program_id(0)
    used = i < nu_ref[0]
    prev = te_ref[jnp.maximum(i - 1, 0)]
    fresh = jnp.logical_or(i == 0, te_ref[i] != prev)

    @pl.when(jnp.logical_and(used, fresh))
    def _():
        wgb[...] = wg_ref[0, 0].astype(bf16)
        wub[...] = wu_ref[0, 0].astype(bf16)
        wdb[...] = wd_ref[0, 0].astype(bf16)

    @pl.when(used)
    def _():
        x = xs_ref[...].astype(bf16)
        h = (jax.nn.silu(jnp.dot(x, wgb[...], preferred_element_type=f32))
             * jnp.dot(x, wub[...], preferred_element_type=f32))
        ys_ref[...] = jnp.dot(h.astype(bf16), wdb[...], preferred_element_type=f32)

    @pl.when(jnp.logical_not(used))
    def _():
        ys_ref[...] = jnp.zeros(ys_ref.shape, ys_ref.dtype)


def _experts(xs, tile_expert, n_used, layer, wg, wu, wd):
    n_rows, d = xs.shape
    tile = EXPERT_TILE
    n_tiles = n_rows // tile
    dff = wg.shape[3]

    def row_map(i, te, nu):
        return (jnp.minimum(i, nu[0] - 1), 0)

    def w_map(i, te, nu):
        return (layer, te[i], 0, 0)

    return pl.pallas_call(
        _expert_kernel,
        grid_spec=pltpu.PrefetchScalarGridSpec(
            num_scalar_prefetch=2,
            grid=(n_tiles,),
            in_specs=[pl.BlockSpec((tile, d), row_map),
                      pl.BlockSpec((1, 1, d, dff), w_map),
                      pl.BlockSpec((1, 1, d, dff), w_map),
                      pl.BlockSpec((1, 1, dff, d), w_map)],
            out_specs=pl.BlockSpec((tile, d), lambda i, te, nu: (i, 0)),
            scratch_shapes=[pltpu.VMEM((d, dff), bf16), pltpu.VMEM((d, dff), bf16),
                            pltpu.VMEM((dff, d), bf16)],
        ),
        out_shape=jax.ShapeDtypeStruct((n_rows, d), f32),
        compiler_params=_cparams(1),
        name="moe_experts",
    )(tile_expert, n_used, xs, wg, wu, wd)


def _combine_kernel(d1_ref, d2_ref, ys_hbm, x1_ref, gate_ref, lng_ref, lnb_ref, out_ref,
                    y1_buf, y2_buf, sems, *, tile, alpha):
    i = pl.program_id(0)
    slot = i % 2

    def row_copy(src, buf, slot_, j):
        return pltpu.make_async_copy(ys_hbm.at[pl.ds(src, 1)], buf.at[slot_, pl.ds(j, 1)],
                                     sems.at[slot_])

    def gather(tile_idx, slot_):
        base = tile_idx * tile

        def start(jq, carry):
            for u in range(DMA_UNROLL):
                j = _aligned(jq * DMA_UNROLL) + u
                row_copy(d1_ref[base + j], y1_buf, slot_, j).start()
                row_copy(d2_ref[base + j], y2_buf, slot_, j).start()
            return carry
        lax.fori_loop(0, tile // DMA_UNROLL, start, 0)

    pl.when(i == 0)(lambda: gather(0, 0))
    pl.when(i + 1 < pl.num_programs(0))(lambda: gather(i + 1, 1 - slot))

    def wait(jq, carry):
        for _ in range(DMA_UNROLL):
            row_copy(0, y1_buf, slot, 0).wait()
            row_copy(0, y2_buf, slot, 0).wait()
        return carry
    lax.fori_loop(0, tile // DMA_UNROLL, wait, 0)

    gates = gate_ref[...]
    moe = gates[:, 0:1] * y1_buf[slot] + gates[:, 1:2] * y2_buf[slot]
    out_ref[...] = _layer_norm(alpha * x1_ref[...] + moe, lng_ref[...], lnb_ref[...])


def _combine(ys, x1, d1, d2, gates, lng, lnb, alpha):
    t, d = x1.shape
    tile = COMBINE_TILE
    assert t % tile == 0
    tok = lambda i, a, b: (i, 0)
    const = lambda i, a, b: (0, 0)
    return pl.pallas_call(
        functools.partial(_combine_kernel, tile=tile, alpha=alpha),
        grid_spec=pltpu.PrefetchScalarGridSpec(
            num_scalar_prefetch=2,
            grid=(t // tile,),
            in_specs=[pl.BlockSpec(memory_space=pl.ANY),
                      pl.BlockSpec((tile, d), tok),
                      pl.BlockSpec((tile, 2), tok),
                      pl.BlockSpec((1, d), const),
                      pl.BlockSpec((1, d), const)],
            out_specs=pl.BlockSpec((tile, d), tok),
            scratch_shapes=[pltpu.VMEM((2, tile, d), f32), pltpu.VMEM((2, tile, d), f32),
                            pltpu.SemaphoreType.DMA((2,))],
        ),
        out_shape=jax.ShapeDtypeStruct((t, d), f32),
        compiler_params=_cparams(1),
        name="moe_combine",
    )(d1, d2, ys, x1, gates, lng, lnb)


def _moe(x1, ri, rf, cnt, layer, wg, wu, wd, lng, lnb, alpha):
    t, d = x1.shape
    tile = EXPERT_TILE
    n_tiles = -(-2 * t // tile) + N_EXPERTS
    counts = cnt[:, 0].astype(i32)
    padded = ((counts + tile - 1) // tile) * tile
    ends = jnp.cumsum(padded)
    starts = ends - padded
    n_used = (ends[-1] // tile).astype(i32).reshape(1)
    d1 = starts[ri[0]] + ri[2]
    d2 = starts[ri[1]] + ri[3]
    tile_start = jnp.minimum(jnp.arange(n_tiles, dtype=i32), n_used[0] - 1) * tile
    tile_expert = jnp.sum(tile_start[:, None] >= ends[None, :], axis=1).astype(i32)
    xs = _dispatch(x1, d1, d2, ends.astype(i32), n_used, n_tiles * tile)
    ys = _experts(xs, tile_expert, n_used, layer, wg, wu, wd)
    gates = jnp.transpose(rf[0:2])
    return _combine(ys, x1, d1, d2, gates, lng, lnb, alpha)


def _rw_proj_kernel(*refs, has_vres, n_ptiles):
    if has_vres:
        (x_ref, hp_ref, hs_ref, sh_ref, vf_ref, mix_ref, wr_ref, wk_ref, wv_ref, w0_ref, w1_ref,
         w2_ref, a0_ref, a1_ref, a2_ref, g1_ref, g2_ref, kk_ref, ka_ref, v0_ref, v1_ref, v2_ref,
         r_out, w_out, k_out, v_out, kk_out, a_out, g_out, xp_buf) = refs
    else:
        (x_ref, hp_ref, hs_ref, sh_ref, mix_ref, wr_ref, wk_ref, wv_ref, w0_ref, w1_ref, w2_ref,
         a0_ref, a1_ref, a2_ref, g1_ref, g2_ref, kk_ref, ka_ref,
         r_out, w_out, k_out, v_out, kk_out, a_out, g_out, xp_buf) = refs
    i = pl.program_id(0)
    tm = x_ref.shape[0]
    bp, bs = hp_ref.shape[0], hs_ref.shape[0]

    @pl.when(i < n_ptiles)
    def _():
        xp_buf[0:bp, :] = jnp.where(i == 0, 0.0, hp_ref[...])
        if tm > bp:
            xp_buf[bp:tm, :] = x_ref[0:tm - bp, :]

    @pl.when(i >= n_ptiles)
    def _():
        xp_buf[0:bs, :] = jnp.where(i == n_ptiles, sh_ref[...], hs_ref[...])
        if tm > bs:
            xp_buf[bs:tm, :] = x_ref[0:tm - bs, :]

    x = x_ref[...]
    xx = xp_buf[...] - x

    def mixed(i):
        return (x + xx * mix_ref[i:i + 1, :]).astype(bf16)

    def proj(wt_ref, xm):
        return lax.dot_general(wt_ref[...], xm, (((1,), (1,)), ((), ())),
                               preferred_element_type=f32)

    def wdot(wt_ref, y):
        return jnp.dot(wt_ref[...], y.astype(bf16), preferred_element_type=f32)

    def col(p_ref):
        return jnp.tile(p_ref[...], (1, tm // LANES))

    r_out[...] = proj(wr_ref, mixed(0))
    lora_w = jnp.tanh(proj(w1_ref, mixed(1)))
    w = -_softplus(-(col(w0_ref) + wdot(w2_ref, lora_w))) - 0.5
    w_out[...] = jnp.exp(-jnp.exp(w))
    k = proj(wk_ref, mixed(2))
    xv = mixed(3)
    v = proj(wv_ref, xv)
    a = jax.nn.sigmoid(col(a0_ref) + wdot(a2_ref, proj(a1_ref, mixed(4))))
    a_out[...] = a
    gate = jax.nn.sigmoid(proj(g1_ref, mixed(5)))
    g_out[...] = wdot(g2_ref, gate)
    kk_out[...] = k * col(kk_ref)
    k_out[...] = k * (1.0 + (a - 1.0) * col(ka_ref))
    if has_vres:
        lora_v = wdot(v2_ref, proj(v1_ref, xv))
        v = v + (vf_ref[...] - v) * jax.nn.sigmoid(col(v0_ref) + lora_v)
    v_out[...] = v


def _rw_proj(x, tp, bp, bs, shift_s, v_first, weights):
    t, d = x.shape
    tm = PROJ_TILE
    assert t % tm == 0 and tp % tm == 0 and tm % bp == 0 and tm % bs == 0
    n_ptiles = tp // tm
    has_vres = v_first is not None
    tile = lambda i: (i, 0)
    const = lambda i: (0, 0)
    halo_p = lambda i: (jnp.maximum(jnp.minimum(i, n_ptiles - 1) * (tm // bp) - 1, 0), 0)
    halo_s = lambda i: (jnp.maximum(i, n_ptiles) * (tm // bs) - 1, 0)
    cols = lambda i: (0, i)
    tok = [x, x, x, shift_s] + ([v_first] if has_vres else [])
    tok_specs = [pl.BlockSpec((tm, d), tile), pl.BlockSpec((bp, d), halo_p),
                 pl.BlockSpec((bs, d), halo_s), pl.BlockSpec((bs, d), const)]
    if has_vres:
        tok_specs.append(pl.BlockSpec((d, tm), cols))
    outs = pl.pallas_call(
        functools.partial(_rw_proj_kernel, has_vres=has_vres, n_ptiles=n_ptiles),
        grid=(t // tm,),
        in_specs=tok_specs + [pl.BlockSpec(w.shape, const) for w in weights],
        out_specs=[pl.BlockSpec((d, tm), cols) for _ in range(7)],
        out_shape=tuple(jax.ShapeDtypeStruct((d, t), f32) for _ in range(7)),
        scratch_shapes=[pltpu.VMEM((tm, d), f32)],
        compiler_params=_cparams(1),
        name="rwkv_proj",
    )(*tok, *weights)
    return outs


def _block_transpose(src, dst, tmp):
    nb = src.shape[0]
    bw = LANES // nb
    blk = lax.broadcasted_iota(i32, tuple(src.shape[1:]), 1) // bw
    for i in range(nb):
        tmp[i] = pltpu.roll(src[i], i * bw, 1) if i else src[i]
    a, b = tmp, dst
    bit = 1
    while bit < nb:
        take = (blk & bit) != 0
        for i in range(nb):
            b[i] = jnp.where(take, a[(i + bit) % nb], a[i])
        a, b = b, a
        bit *= 2
    assert a is tmp
    for i in range(nb):
        dst[(nb - i) % nb] = pltpu.roll(tmp[i], i * bw, 1) if i else tmp[i]


def _wkv_kernel(r_ref, w_ref, k_ref, v_ref, kk_ref, a_ref, s0_ref, lng_ref, lnb_ref, rk_ref,
                o_ref, s_ref, cur, o_buf, tmp_buf, *, steps, time_on_lanes):
    n = s_ref.shape[0]
    R, W, K, V, KK, A = range(6)
    srcs = (r_ref, w_ref, k_ref, v_ref, kk_ref, a_ref)

    @pl.when(pl.program_id(1) == 0)
    def _():
        s_ref[...] = s0_ref[...]

    zeros = jnp.zeros((n, LANES), f32)

    def bcast(q, t, kidx):
        row = cur[q, t, pl.ds(kidx, 1), :]
        return jnp.broadcast_to(row, (n, LANES))

    def prep(t, carry):
        kk_raw = cur[KK, t]
        norm = jnp.sqrt(jnp.sum(kk_raw * kk_raw, axis=0, keepdims=True))
        kk = kk_raw / jnp.maximum(norm, 1e-12)
        cur[KK, t] = -kk
        cur[A, t] = kk * cur[A, t]
        return carry

    def state_dot(t):
        def body(kq, sa):
            for u in range(SUBLANES):
                kidx = kq * SUBLANES + u
                sa = sa + s_ref[kidx] * bcast(KK, t, kidx)
            return sa
        return lax.fori_loop(0, n // SUBLANES, body, zeros)

    def advance(t, sa, look_ahead):
        vv = cur[V, t]

        def body(kq, carry):
            out, sa_next = carry
            for u in range(SUBLANES):
                kidx = kq * SUBLANES + u
                s_new = (s_ref[kidx] * bcast(W, t, kidx) + sa * bcast(A, t, kidx)
                         + vv * bcast(K, t, kidx))
                s_ref[kidx] = s_new
                out = out + s_new * bcast(R, t, kidx)
                if look_ahead:
                    sa_next = sa_next + s_new * bcast(KK, t + 1, kidx)
            return out, sa_next
        out, sa_next = lax.fori_loop(0, n // SUBLANES, body, (zeros, zeros))

        mu = jnp.mean(out, axis=0, keepdims=True)
        dlt = out - mu
        var = jnp.mean(dlt * dlt, axis=0, keepdims=True)
        o_n = dlt * lax.rsqrt(var + GN_EPS) * lng_ref[...] + lnb_ref[...]
        bonus = jnp.sum(cur[R, t] * cur[K, t] * rk_ref[...], axis=0, keepdims=True) * vv
        o_buf[t] = o_n + bonus
        return sa_next

    for q, ref in enumerate(srcs):
        if time_on_lanes:
            _block_transpose(ref, cur.at[q], tmp_buf)
        else:
            for t in range(steps):
                cur[q, t] = ref[0, :, t * LANES:(t + 1) * LANES]

    lax.fori_loop(0, steps, prep, 0)
    sa = state_dot(0)
    if steps > 1:
        sa = lax.fori_loop(0, steps - 1, lambda t, s: advance(t, s, True), sa)
    advance(steps - 1, sa, False)

    if time_on_lanes:
        _block_transpose(o_buf, o_ref, tmp_buf)
    else:
        for t in range(steps):
            o_ref[0, :, t * LANES:(t + 1) * LANES] = o_buf[t]


def _wkv_scan(streams, col0, n_steps, bn, s0, lng, lnb, rk):
    d = streams[0].shape[0]
    n = HEAD_DIM
    heads = d // n
    time_on_lanes = bn * heads == LANES
    if time_on_lanes:
        steps = LANES // bn
        assert n_steps % steps == 0 and col0 % LANES == 0
        grid = (1, n_steps // steps)
        blk = (heads, n, LANES)
        c0 = col0 // LANES
        in_map = lambda j, t: (0, 0, c0 + t)
        out_map = lambda j, t: (0, 0, t)
    else:
        steps = n_steps
        assert bn == LANES and col0 % (steps * LANES) == 0
        grid = (heads, 1)
        blk = (1, n, steps * LANES)
        c0 = col0 // (steps * LANES)
        in_map = lambda j, t: (j, 0, c0)
        out_map = lambda j, t: (j, 0, 0)
    per_lane = lambda j, t: (0, 0, j)
    par = lambda j, t: (0, j)
    views = [s.reshape(heads, n, s.shape[1]) for s in streams]
    o, s_fin = pl.pallas_call(
        functools.partial(_wkv_kernel, steps=steps, time_on_lanes=time_on_lanes),
        grid=grid,
        in_specs=([pl.BlockSpec(blk, in_map) for _ in range(6)]
                  + [pl.BlockSpec((n, n, LANES), per_lane)]
                  + [pl.BlockSpec((n, LANES), par) for _ in range(3)]),
        out_specs=[pl.BlockSpec(blk, out_map),
                   pl.BlockSpec((n, n, LANES), per_lane)],
        out_shape=(jax.ShapeDtypeStruct((heads, n, n_steps * bn), f32),
                   jax.ShapeDtypeStruct((n, n, heads * bn), f32)),
        scratch_shapes=[pltpu.VMEM((6, steps, n, LANES), f32), pltpu.VMEM((steps, n, LANES), f32),
                        pltpu.VMEM((steps, n, LANES), f32)],
        compiler_params=_cparams(2),
        name="wkv_scan",
    )(*views, s0, lng, lnb, rk)
    return o.reshape(d, n_steps * bn), s_fin


def _reorder_plan(shape_p, shape_s):
    bp, lp, d = shape_p
    bs, ls, _ = shape_s
    rows = bs * ls
    assert rows % bp == 0 and (bp * lp) % rows == 0
    return bp, bs, ls, d, rows, rows // bp, (bp * lp) // rows


def _to_time_major_kernel(xp_ref, xs_ref, o_ref, *, n_pblocks):
    bp, tb, _ = xp_ref.shape
    bs, ls, _ = xs_ref.shape

    @pl.when(pl.program_id(0) < n_pblocks)
    def _():
        for t in range(tb):
            o_ref[t * bp:(t + 1) * bp, :] = xp_ref[:, t, :]

    @pl.when(pl.program_id(0) >= n_pblocks)
    def _():
        for t in range(ls):
            o_ref[t * bs:(t + 1) * bs, :] = xs_ref[:, t, :]


def _to_time_major(x_prompt, x_sample):
    bp, bs, ls, d, rows, tb, n_pblocks = _reorder_plan(x_prompt.shape, x_sample.shape)
    return pl.pallas_call(
        functools.partial(_to_time_major_kernel, n_pblocks=n_pblocks),
        grid=(n_pblocks + 1,),
        in_specs=[pl.BlockSpec((bp, tb, d), lambda i: (0, jnp.minimum(i, n_pblocks - 1), 0)),
                  pl.BlockSpec((bs, ls, d), lambda i: (0, 0, 0))],
        out_specs=pl.BlockSpec((rows, d), lambda i: (i, 0)),
        out_shape=jax.ShapeDtypeStruct(((n_pblocks + 1) * rows, d), x_prompt.dtype),
        compiler_params=_cparams(1),
        name="to_time_major",
    )(x_prompt, x_sample)


def _to_batch_major_kernel(x_ref, yp_ref, ys_ref, *, n_pblocks):
    bp, tb, _ = yp_ref.shape
    bs, ls, _ = ys_ref.shape

    @pl.when(pl.program_id(0) < n_pblocks)
    def _():
        for t in range(tb):
            yp_ref[:, t, :] = x_ref[t * bp:(t + 1) * bp, :]

    @pl.when(pl.program_id(0) >= n_pblocks)
    def _():
        for t in range(ls):
            ys_ref[:, t, :] = x_ref[t * bs:(t + 1) * bs, :]


def _to_batch_major(x, shape_p, shape_s):
    bp, bs, ls, d, rows, tb, n_pblocks = _reorder_plan(shape_p, shape_s)
    return pl.pallas_call(
        functools.partial(_to_batch_major_kernel, n_pblocks=n_pblocks),
        grid=(n_pblocks + 1,),
        in_specs=[pl.BlockSpec((rows, d), lambda i: (i, 0))],
        out_specs=[pl.BlockSpec((bp, tb, d), lambda i: (0, jnp.minimum(i, n_pblocks - 1), 0)),
                   pl.BlockSpec((bs, ls, d), lambda i: (0, 0, 0))],
        out_shape=(jax.ShapeDtypeStruct(shape_p, x.dtype), jax.ShapeDtypeStruct(shape_s, x.dtype)),
        compiler_params=_cparams(1),
        name="to_batch_major",
    )(x)


def _lanes_param(p, bn):
    h = p.size // HEAD_DIM
    y = jnp.transpose(p.reshape(h, HEAD_DIM))
    return jnp.repeat(y, bn, axis=1)


def _col_param(p):
    return jnp.broadcast_to(p.reshape(-1, 1), (p.size, LANES))


def _pad_cols(w, mult=LANES):
    pad = (-w.shape[1]) % mult
    return jnp.pad(w, ((0, 0), (0, pad)))


def _pad_rows(w, mult=LANES):
    pad = (-w.shape[0]) % mult
    return jnp.pad(w, ((0, pad), (0, 0)))


def kernel(x_prompt, x_sample, state_s5_re, state_s5_im, state_wkv, state_shift, ln_gain, ln_bias,
           s5_lambda_re, s5_lambda_im, s5_log_step, s5_b_re, s5_b_im, s5_c_re, s5_c_im, s5_d,
           s5_w_out, s5_w_gate, rw_mix, rw_w_r, rw_w_k, rw_w_v, rw_w_o, rw_w0, rw_w1, rw_w2,
           rw_a0, rw_a1, rw_a2, rw_v0, rw_v1, rw_v2, rw_g1, rw_g2, rw_k_k, rw_k_a, rw_r_k,
           rw_lnx_g, rw_lnx_b, w_router, router_bias, moe_w_gate, moe_w_up, moe_w_down):
    bp, lp, d = x_prompt.shape
    bs, ls, _ = x_sample.shape
    tp, ts = bp * lp, bs * ls
    depth = ln_gain.shape[0]
    alpha = (2.0 * depth) ** 0.25
    n_groups = s5_lambda_re.shape[1]
    n_state = s5_lambda_re.shape[2]
    half = n_groups * n_state
    heads = d // HEAD_DIM

    x = _to_time_major(x_prompt, x_sample)

    wr_t = jnp.transpose(w_router)
    wr_hi = wr_t.astype(bf16)
    wr_lo = (wr_t - wr_hi.astype(f32)).astype(bf16)
    rbias = router_bias.reshape(N_EXPERTS, 1)
    tri = jnp.triu(jnp.ones((TOK_TILE, TOK_TILE), bf16), k=1)
    router = (wr_hi, wr_lo, rbias, tri)

    new_re_p, new_im_p, new_re_s, new_im_s = [], [], [], []
    new_wkv_p, new_wkv_s, new_shift_p, new_shift_s = [], [], [], []
    v_first = None
    groups_per_slab = MXU_DIM // S5_GROUP

    for i in range(depth):
        j = i // 2
        if i % 2 == 0:
            a_re, a_im, bb_re, bb_im = _s5_discretise(s5_lambda_re[j], s5_lambda_im[j], s5_log_step[j],
                                                      s5_b_re[j], s5_b_im[j])
            a_re8 = jnp.broadcast_to(a_re.reshape(1, half), (SUBLANES, half))
            a_im8 = jnp.broadcast_to(a_im.reshape(1, half), (SUBLANES, half))
            n_slab = n_groups // groups_per_slab

            def pack_b(bb):
                w = jnp.transpose(bb, (1, 0, 2)).reshape(n_slab, groups_per_slab, S5_GROUP, n_state)
                return _block_diag(w, groups_per_slab)
            wb = jnp.concatenate([pack_b(bb_re), pack_b(bb_im)], axis=2).astype(bf16)

            def pack_c(c):
                w = jnp.transpose(c, (0, 2, 1)).reshape(n_slab, groups_per_slab, n_state, S5_GROUP)
                return _block_diag(w, groups_per_slab).astype(bf16)
            wc_re, wc_im = pack_c(s5_c_re[j]), pack_c(s5_c_im[j])
            d_skip = s5_d[j].reshape(1, d)

            zero_h = jnp.zeros((bp, half), f32)
            z_p, hre_p, him_p = _s5_scan(x, 0, lp, bp, zero_h, zero_h, a_re8, a_im8,
                                         wb, wc_re, wc_im, d_skip)
            z_s, hre_s, him_s = _s5_scan(x, tp, ls, bs, state_s5_re[j].reshape(bs, half),
                                         state_s5_im[j].reshape(bs, half), a_re8, a_im8,
                                         wb, wc_re, wc_im, d_skip)
            new_re_p.append(hre_p.reshape(bp, n_groups, n_state))
            new_im_p.append(him_p.reshape(bp, n_groups, n_state))
            new_re_s.append(hre_s.reshape(bs, n_groups, n_state))
            new_im_s.append(him_s.reshape(bs, n_groups, n_state))
            x1, ri, rf, cnt = _post_mixer(
                _s5_out_kernel, "s5_out", z_p, z_s, [x],
                [s5_w_out[j].astype(bf16), s5_w_gate[j].astype(bf16)],
                ln_gain[i, 0].reshape(1, d), ln_bias[i, 0].reshape(1, d), router, alpha)
        else:
            new_shift_p.append(x[tp - bp:tp])
            new_shift_s.append(x[tp + ts - bs:tp + ts])
            wt = lambda w: jnp.transpose(w).astype(bf16)
            weights = [rw_mix[j],
                       wt(rw_w_r[j]), wt(rw_w_k[j]), wt(rw_w_v[j]),
                       _col_param(rw_w0[j]), wt(_pad_cols(rw_w1[j])), wt(_pad_rows(rw_w2[j])),
                       _col_param(rw_a0[j]), wt(_pad_cols(rw_a1[j])), wt(_pad_rows(rw_a2[j])),
                       wt(_pad_cols(rw_g1[j], MXU_DIM)), wt(_pad_rows(rw_g2[j], MXU_DIM)),
                       _col_param(rw_k_k[j]), _col_param(rw_k_a[j])]
            if j > 0:
                weights += [_col_param(rw_v0[j - 1]), wt(_pad_cols(rw_v1[j - 1])),
                            wt(_pad_rows(rw_v2[j - 1]))]
            r, w, k, v, kk, a, g = _rw_proj(x, tp, bp, bs, state_shift[j],
                                            v_first if j > 0 else None, weights)
            if j == 0:
                v_first = v

            def run_group(col0, n_steps, bn, s0):
                o, s_fin = _wkv_scan((r, w, k, v, kk, a), col0, n_steps, bn, s0,
                                     _lanes_param(rw_lnx_g[j], bn), _lanes_param(rw_lnx_b[j], bn),
                                     _lanes_param(rw_r_k[j], bn))
                s_fin = jnp.transpose(s_fin.reshape(HEAD_DIM, HEAD_DIM, heads, bn), (3, 2, 1, 0))
                return o, s_fin

            o_p, s_p = run_group(0, lp, bp, jnp.zeros((HEAD_DIM, HEAD_DIM, heads * bp), f32))
            s0_s = jnp.transpose(state_wkv[j], (3, 2, 1, 0)).reshape(HEAD_DIM, HEAD_DIM, heads * bs)
            o_s, s_s = run_group(tp, ls, bs, s0_s)
            new_wkv_p.append(s_p)
            new_wkv_s.append(s_s)
            x1, ri, rf, cnt = _post_mixer(
                _rw_out_kernel, "rwkv_out", o_p, o_s, [g, x], [wt(rw_w_o[j])],
                ln_gain[i, 0].reshape(1, d), ln_bias[i, 0].reshape(1, d), router, alpha,
                channel_major=True)

        x = _moe(x1, ri, rf, cnt, i, moe_w_gate, moe_w_up, moe_w_down,
                 ln_gain[i, 1].reshape(1, d), ln_bias[i, 1].reshape(1, d), alpha)

    y_prompt, y_sample = _to_batch_major(x, x_prompt.shape, x_sample.shape)
    return (y_prompt, y_sample,
            jnp.stack(new_re_p), jnp.stack(new_im_p), jnp.stack(new_wkv_p), jnp.stack(new_shift_p),
            jnp.stack(new_re_s), jnp.stack(new_im_s), jnp.stack(new_wkv_s), jnp.stack(new_shift_s))
```

```python
import functools
import math

import jax
import jax.numpy as jnp
from jax import lax
from jax.experimental import pallas as pl
from jax.experimental.pallas import tpu as pltpu

f32 = jnp.float32
bf16 = jnp.bfloat16
i32 = jnp.int32

S5_GROUP = 16
S5_STATE = 64
HEAD_DIM = 64
GN_EPS = 64e-5
LN_EPS = 1e-5
N_EXPERTS = 16
N_EXPERT_GROUPS = 4
EXPERTS_PER_GROUP = N_EXPERTS // N_EXPERT_GROUPS

SUBLANES = 8
LANES = 128
MXU_DIM = 256
VMEM_LIMIT = 56 * 1024 * 1024

TOK_TILE = 512
PROJ_TILE = 256
S5_ROWS = 256
EXPERT_TILE = 512
COMBINE_TILE = 256
DISPATCH_TILE = 512
DMA_UNROLL = 8


def _cparams(n_axes=1, **kw):
    return pltpu.CompilerParams(dimension_semantics=("arbitrary",) * n_axes,
                                vmem_limit_bytes=VMEM_LIMIT, **kw)


def _bdot(a, w):
    return jnp.dot(a.astype(bf16), w, preferred_element_type=f32)


def _layer_norm(y, g, b):
    mu = jnp.mean(y, axis=-1, keepdims=True)
    d = y - mu
    var = jnp.mean(d * d, axis=-1, keepdims=True)
    return d * lax.rsqrt(var + LN_EPS) * g + b


def _aligned(idx):
    return idx if isinstance(idx, int) else pl.multiple_of(idx, SUBLANES)


def _softplus(x):
    return jnp.maximum(x, 0.0) + jnp.log1p(jnp.exp(-jnp.abs(x)))


def _route(x1, wr_hi, wr_lo, rbias, tri, cnt_ref, ri_ref, rf_ref, cnt_out_ref):
    tm = x1.shape[0]
    xh = x1.astype(bf16)
    xl = (x1 - xh.astype(f32)).astype(bf16)
    dn = (((1,), (1,)), ((), ()))
    logits = (lax.dot_general(wr_hi, xh, dn, preferred_element_type=f32)
              + lax.dot_general(wr_lo, xh, dn, preferred_element_type=f32)
              + lax.dot_general(wr_hi, xl, dn, preferred_element_type=f32))
    m = jnp.max(logits, axis=0, keepdims=True)
    ex = jnp.exp(logits - m)
    probs = ex / jnp.sum(ex, axis=0, keepdims=True)
    sel = probs + rbias
    s = [sel[e:e + 1, :] for e in range(N_EXPERTS)]
    p = [probs[e:e + 1, :] for e in range(N_EXPERTS)]

    best = None
    gi = None
    for g in range(N_EXPERT_GROUPS):
        a, b, c, d = s[4 * g:4 * g + 4]
        hi1, lo1 = jnp.maximum(a, b), jnp.minimum(a, b)
        hi2, lo2 = jnp.maximum(c, d), jnp.minimum(c, d)
        score = jnp.maximum(hi1, hi2) + jnp.maximum(jnp.minimum(hi1, hi2), jnp.maximum(lo1, lo2))
        if g == 0:
            best, gi = score, jnp.zeros(score.shape, i32)
        else:
            better = score > best
            best = jnp.where(better, score, best)
            gi = jnp.where(better, g, gi)

    def in_group(rows, j):
        out = rows[12 + j]
        for g in (2, 1, 0):
            out = jnp.where(gi == g, rows[4 * g + j], out)
        return out

    v = [in_group(s, j) for j in range(EXPERTS_PER_GROUP)]
    pv = [in_group(p, j) for j in range(EXPERTS_PER_GROUP)]
    order = []
    for j in range(EXPERTS_PER_GROUP):
        r = jnp.zeros(v[j].shape, i32)
        for i in range(EXPERTS_PER_GROUP):
            if i < j:
                r = r + jnp.where(v[i] >= v[j], 1, 0)
            elif i > j:
                r = r + jnp.where(v[i] > v[j], 1, 0)
        order.append(r)
    j1 = sum(jnp.where(order[j] == 0, j, 0) for j in range(EXPERTS_PER_GROUP))
    j2 = sum(jnp.where(order[j] == 1, j, 0) for j in range(EXPERTS_PER_GROUP))
    p1 = sum(jnp.where(order[j] == 0, pv[j], 0.0) for j in range(EXPERTS_PER_GROUP))
    p2 = sum(jnp.where(order[j] == 1, pv[j], 0.0) for j in range(EXPERTS_PER_GROUP))
    e1 = gi * EXPERTS_PER_GROUP + j1
    e2 = gi * EXPERTS_PER_GROUP + j2
    den = p1 + p2
    g1 = p1 / den
    g2 = p2 / den

    eio = lax.broadcasted_iota(i32, (N_EXPERTS, tm), 0)
    oh1 = eio == e1
    oh2 = eio == e2
    oh = jnp.where(oh1, 1.0, 0.0) + jnp.where(oh2, 1.0, 0.0)
    before = jnp.dot(oh.astype(bf16), tri, preferred_element_type=f32)
    before = before + cnt_ref[:, 0:1]
    r1 = jnp.sum(jnp.where(oh1, before, 0.0), axis=0, keepdims=True)
    r2 = jnp.sum(jnp.where(oh2, before, 0.0), axis=0, keepdims=True)
    cnt_new = cnt_ref[...] + jnp.sum(oh, axis=1, keepdims=True)
    cnt_ref[...] = cnt_new
    cnt_out_ref[...] = cnt_new

    ri_ref[...] = jnp.zeros(ri_ref.shape, i32)
    rf_ref[...] = jnp.zeros(rf_ref.shape, f32)
    ri_ref[0:1, :] = e1
    ri_ref[1:2, :] = e2
    ri_ref[2:3, :] = r1.astype(i32)
    ri_ref[3:4, :] = r2.astype(i32)
    rf_ref[0:1, :] = g1
    rf_ref[1:2, :] = g2


def _s5_disc_kernel(lre_ref, lim_ref, lstep_ref, bre_ref, bim_ref,
                    are_ref, aim_ref, bbre_ref, bbim_ref):
    lre = lre_ref[...]
    lim = lim_ref[...]
    step = jnp.exp(lstep_ref[...])
    mag = jnp.exp(lre * step)
    th = lim * step
    a_re = mag * jnp.cos(th)
    a_im = mag * jnp.sin(th)
    den = lre * lre + lim * lim
    q_re = ((a_re - 1.0) * lre + a_im * lim) / den
    q_im = (a_im * lre - (a_re - 1.0) * lim) / den
    are_ref[...] = a_re
    aim_ref[...] = a_im
    b_re = bre_ref[...]
    b_im = bim_ref[...]
    bbre_ref[...] = q_re[None] * b_re - q_im[None] * b_im
    bbim_ref[...] = q_re[None] * b_im + q_im[None] * b_re


def _s5_discretise(lam_re, lam_im, log_step, b_re, b_im):
    g, p = lam_re.shape
    c = b_re.shape[-1]
    outs = pl.pallas_call(
        _s5_disc_kernel,
        out_shape=(jax.ShapeDtypeStruct((g, p), f32), jax.ShapeDtypeStruct((g, p), f32),
                   jax.ShapeDtypeStruct((c, g, p), f32), jax.ShapeDtypeStruct((c, g, p), f32)),
        name="s5_discretise",
    )(lam_re, lam_im, log_step.reshape(g, 1),
      jnp.transpose(b_re, (2, 0, 1)), jnp.transpose(b_im, (2, 0, 1)))
    return outs


def _block_diag(w, n_blk):
    s, _, a, b = w.shape
    eye = jnp.eye(n_blk, dtype=w.dtype)
    out = w[:, :, :, None, :] * eye[None, :, None, :, None]
    return out.reshape(s, n_blk * a, n_blk * b)


def _s5_kernel(u_ref, h0re_ref, h0im_ref, are_ref, aim_ref, wb_ref, wcre_ref, wcim_ref, d_ref,
               z_ref, hre_ref, him_ref, bu_ref, *, bn, tb):
    i = pl.program_id(0)
    half = are_ref.shape[1]
    n_slab = wb_ref.shape[0]
    kw = wb_ref.shape[1]
    sw = half // n_slab
    cw = 1024

    @pl.when(i == 0)
    def _():
        hre_ref[...] = h0re_ref[...]
        him_ref[...] = h0im_ref[...]

    ub = u_ref[...].astype(bf16)
    for s in range(n_slab):
        res = jnp.dot(ub[:, s * kw:(s + 1) * kw], wb_ref[s], preferred_element_type=f32)
        bu_ref[:, s * sw:(s + 1) * sw] = res[:, :sw]
        bu_ref[:, half + s * sw:half + (s + 1) * sw] = res[:, sw:]

    def advance(t, first):
        def sub(j, carry):
            rr = _aligned(t * bn + j * SUBLANES)
            for c0 in range(0, half, cw):
                if first:
                    jr = _aligned(j * SUBLANES)
                    p_re = hre_ref[pl.ds(jr, SUBLANES), c0:c0 + cw]
                    p_im = him_ref[pl.ds(jr, SUBLANES), c0:c0 + cw]
                else:
                    pr = _aligned(rr - bn)
                    p_re = bu_ref[pl.ds(pr, SUBLANES), c0:c0 + cw]
                    p_im = bu_ref[pl.ds(pr, SUBLANES), half + c0:half + c0 + cw]
                a_re = are_ref[:, c0:c0 + cw]
                a_im = aim_ref[:, c0:c0 + cw]
                n_re = a_re * p_re - a_im * p_im + bu_ref[pl.ds(rr, SUBLANES), c0:c0 + cw]
                n_im = a_re * p_im + a_im * p_re + bu_ref[pl.ds(rr, SUBLANES), half + c0:half + c0 + cw]
                bu_ref[pl.ds(rr, SUBLANES), c0:c0 + cw] = n_re
                bu_ref[pl.ds(rr, SUBLANES), half + c0:half + c0 + cw] = n_im
            return carry
        if bn == SUBLANES:
            sub(0, 0)
        else:
            lax.fori_loop(0, bn // SUBLANES, sub, 0)

    advance(0, True)
    if tb > 1:
        def body(t, carry):
            advance(t, False)
            return carry
        lax.fori_loop(1, tb, body, 0)
    hre_ref[...] = bu_ref[(tb - 1) * bn:tb * bn, 0:half]
    him_ref[...] = bu_ref[(tb - 1) * bn:tb * bn, half:2 * half]

    nw = wcre_ref.shape[2]
    for n in range(wcre_ref.shape[0]):
        h_re = bu_ref[:, n * sw:(n + 1) * sw]
        h_im = bu_ref[:, half + n * sw:half + (n + 1) * sw]
        y = _bdot(h_re, wcre_ref[n]) - _bdot(h_im, wcim_ref[n])
        y = y + d_ref[:, n * nw:(n + 1) * nw] * u_ref[:, n * nw:(n + 1) * nw]
        z_ref[:, n * nw:(n + 1) * nw] = jax.nn.gelu(y, approximate=True).astype(z_ref.dtype)


def _s5_scan(x, row0, n_steps, bn, h0_re, h0_im, a_re8, a_im8, wb, wc_re, wc_im, d_skip):
    d = x.shape[1]
    tb = max(1, S5_ROWS // bn)
    rows = tb * bn
    assert n_steps % tb == 0 and row0 % rows == 0
    half = a_re8.shape[1]
    blk0 = row0 // rows
    const2 = lambda i: (0, 0)
    const3 = lambda i: (0, 0, 0)
    z, h_re, h_im = pl.pallas_call(
        functools.partial(_s5_kernel, bn=bn, tb=tb),
        grid=(n_steps // tb,),
        in_specs=[
            pl.BlockSpec((rows, d), lambda i: (i + blk0, 0)),
            pl.BlockSpec((bn, half), const2),
            pl.BlockSpec((bn, half), const2),
            pl.BlockSpec(a_re8.shape, const2),
            pl.BlockSpec(a_im8.shape, const2),
            pl.BlockSpec(wb.shape, const3),
            pl.BlockSpec(wc_re.shape, const3),
            pl.BlockSpec(wc_im.shape, const3),
            pl.BlockSpec((1, d), const2),
        ],
        out_specs=[
            pl.BlockSpec((rows, d), lambda i: (i, 0)),
            pl.BlockSpec((bn, half), const2),
            pl.BlockSpec((bn, half), const2),
        ],
        out_shape=(jax.ShapeDtypeStruct((n_steps * bn, d), bf16),
                   jax.ShapeDtypeStruct((bn, half), f32),
                   jax.ShapeDtypeStruct((bn, half), f32)),
        scratch_shapes=[pltpu.VMEM((rows, 2 * half), f32)],
        compiler_params=_cparams(1),
        name="s5_scan",
    )(x, h0_re, h0_im, a_re8, a_im8, wb, wc_re, wc_im, d_skip)
    return z, h_re, h_im


def _s5_out_kernel(zp_ref, zs_ref, x_ref, wo_ref, wg_ref, lng_ref, lnb_ref,
                   wrh_ref, wrl_ref, rb_ref, tri_ref,
                   x1_ref, ri_ref, rf_ref, cnt_out_ref, cnt_ref, *, alpha, n_ptiles):
    @pl.when(pl.program_id(0) == 0)
    def _():
        cnt_ref[...] = jnp.zeros(cnt_ref.shape, f32)

    z = jnp.where(pl.program_id(0) < n_ptiles, zp_ref[...], zs_ref[...])
    mixed = (jnp.dot(z, wo_ref[...], preferred_element_type=f32)
             * jax.nn.sigmoid(jnp.dot(z, wg_ref[...], preferred_element_type=f32)))
    x1 = _layer_norm(alpha * x_ref[...] + mixed, lng_ref[...], lnb_ref[...])
    x1_ref[...] = x1
    _route(x1, wrh_ref[...], wrl_ref[...], rb_ref[...], tri_ref[...], cnt_ref, ri_ref, rf_ref,
           cnt_out_ref)


def _rw_out_kernel(op_ref, os_ref, g_ref, x_ref, wo_ref, lng_ref, lnb_ref,
                   wrh_ref, wrl_ref, rb_ref, tri_ref,
                   x1_ref, ri_ref, rf_ref, cnt_out_ref, cnt_ref, *, alpha, n_ptiles):
    @pl.when(pl.program_id(0) == 0)
    def _():
        cnt_ref[...] = jnp.zeros(cnt_ref.shape, f32)

    o = jnp.where(pl.program_id(0) < n_ptiles, op_ref[...], os_ref[...])
    mixed_t = jnp.dot(wo_ref[...], (o * g_ref[...]).astype(bf16), preferred_element_type=f32)
    mixed = jnp.transpose(mixed_t)
    x1 = _layer_norm(alpha * x_ref[...] + mixed, lng_ref[...], lnb_ref[...])
    x1_ref[...] = x1
    _route(x1, wrh_ref[...], wrl_ref[...], rb_ref[...], tri_ref[...], cnt_ref, ri_ref, rf_ref,
           cnt_out_ref)


def _post_mixer(kernel_fn, name, mix_p, mix_s, tok_inputs, weights, lng, lnb, router, alpha,
                channel_major=False):
    t, d = tok_inputs[-1].shape
    tm = TOK_TILE
    tok_axis = 1 if channel_major else 0
    assert t % tm == 0 and mix_p.shape[tok_axis] % tm == 0 and mix_s.shape[tok_axis] % tm == 0
    n_ptiles = mix_p.shape[tok_axis] // tm
    wr_hi, wr_lo, rbias, tri = router
    tile = lambda i: (i, 0)
    const = lambda i: (0, 0)
    if channel_major:
        blk = (d, tm)
        at = lambda f: (lambda i: (0, f(i)))
    else:
        blk = (tm, d)
        at = lambda f: (lambda i: (f(i), 0))
    in_specs = ([pl.BlockSpec(blk, at(lambda i: jnp.minimum(i, n_ptiles - 1))),
                 pl.BlockSpec(blk, at(lambda i: jnp.maximum(i - n_ptiles, 0)))]
                + [pl.BlockSpec(blk, at(lambda i: i)) for _ in tok_inputs[:-1]]
                + [pl.BlockSpec((tm, d), tile)]
                + [pl.BlockSpec(w.shape, const) for w in weights]
                + [pl.BlockSpec((1, d), const), pl.BlockSpec((1, d), const),
                   pl.BlockSpec(wr_hi.shape, const), pl.BlockSpec(wr_lo.shape, const),
                   pl.BlockSpec(rbias.shape, const), pl.BlockSpec(tri.shape, const)])
    x1, ri, rf, cnt = pl.pallas_call(
        functools.partial(kernel_fn, alpha=alpha, n_ptiles=n_ptiles),
        grid=(t // tm,),
        in_specs=in_specs,
        out_specs=[pl.BlockSpec((tm, d), tile),
                   pl.BlockSpec((SUBLANES, tm), lambda i: (0, i)),
                   pl.BlockSpec((SUBLANES, tm), lambda i: (0, i)),
                   pl.BlockSpec((N_EXPERTS, LANES), const)],
        out_shape=(jax.ShapeDtypeStruct((t, d), f32),
                   jax.ShapeDtypeStruct((SUBLANES, t), i32),
                   jax.ShapeDtypeStruct((SUBLANES, t), f32),
                   jax.ShapeDtypeStruct((N_EXPERTS, LANES), f32)),
        scratch_shapes=[pltpu.VMEM((N_EXPERTS, LANES), f32)],
        compiler_params=_cparams(1),
        name=name,
    )(mix_p, mix_s, *tok_inputs, *weights, lng, lnb, wr_hi, wr_lo, rbias, tri)
    return x1, ri, rf, cnt


def _dispatch_kernel(d1_ref, d2_ref, ends_ref, nu_ref, x_ref, xs_hbm, zero_buf, sem, zsem, *,
                     tile, ztile, n_ztiles):
    base = pl.program_id(0) * tile

    @pl.when(pl.program_id(0) == 0)
    def _():
        zero_buf[...] = jnp.zeros(zero_buf.shape, zero_buf.dtype)

        def zero_tile(idx):
            return pltpu.make_async_copy(zero_buf, xs_hbm.at[pl.ds(idx * ztile, ztile)], zsem)

        def each(fn):
            for e in range(N_EXPERTS):
                begin = ends_ref[e - 1] if e else 0
                pl.when(ends_ref[e] > begin)(lambda e=e: fn(ends_ref[e] // ztile - 1))

            def tail(idx, carry):
                pl.when(idx >= nu_ref[0])(lambda: fn(idx))
                return carry
            lax.fori_loop(0, n_ztiles, tail, 0)
        each(lambda idx: zero_tile(idx).start())
        each(lambda idx: zero_tile(idx).wait())

    def row_copy(j, dst):
        return pltpu.make_async_copy(x_ref.at[pl.ds(j, 1)], xs_hbm.at[pl.ds(dst, 1)], sem)

    def start(jq, carry):
        for u in range(DMA_UNROLL):
            j = _aligned(jq * DMA_UNROLL) + u
            row_copy(j, d1_ref[base + j]).start()
            row_copy(j, d2_ref[base + j]).start()
        return carry
    lax.fori_loop(0, tile // DMA_UNROLL, start, 0)

    def wait(jq, carry):
        for _ in range(2 * DMA_UNROLL):
            row_copy(0, 0).wait()
        return carry
    lax.fori_loop(0, tile // DMA_UNROLL, wait, 0)


def _dispatch(x1, d1, d2, ends, n_used, n_rows):
    t, d = x1.shape
    tile = DISPATCH_TILE
    ztile = EXPERT_TILE
    assert t % tile == 0 and n_rows % ztile == 0
    return pl.pallas_call(
        functools.partial(_dispatch_kernel, tile=tile, ztile=ztile, n_ztiles=n_rows // ztile),
        grid_spec=pltpu.PrefetchScalarGridSpec(
            num_scalar_prefetch=4,
            grid=(t // tile,),
            in_specs=[pl.BlockSpec((tile, d), lambda i, *_: (i, 0))],
            out_specs=pl.BlockSpec(memory_space=pl.ANY),
            scratch_shapes=[pltpu.VMEM((ztile, d), x1.dtype),
                            pltpu.SemaphoreType.DMA(()), pltpu.SemaphoreType.DMA(())],
        ),
        out_shape=jax.ShapeDtypeStruct((n_rows, d), x1.dtype),
        compiler_params=_cparams(1, has_side_effects=True),
        name="moe_dispatch",
    )(d1, d2, ends, n_used, x1)


def _expert_kernel(te_ref, nu_ref, xs_ref, wg_ref, wu_ref, wd_ref, ys_ref, wgb, wub, wdb):
    i = pl.program_id(0)
    used = i < nu_ref[0]
    prev = te_ref[jnp.maximum(i - 1, 0)]
    fresh = jnp.logical_or(i == 0, te_ref[i] != prev)

    @pl.when(jnp.logical_and(used, fresh))
    def _():
        wgb[...] = wg_ref[0, 0].astype(bf16)
        wub[...] = wu_ref[0, 0].astype(bf16)
        wdb[...] = wd_ref[0, 0].astype(bf16)

    @pl.when(used)
    def _():
        x = xs_ref[...].astype(bf16)
        h = (jax.nn.silu(jnp.dot(x, wgb[...], preferred_element_type=f32))
             * jnp.dot(x, wub[...], preferred_element_type=f32))
        ys_ref[...] = jnp.dot(h.astype(bf16), wdb[...], preferred_element_type=f32)

    @pl.when(jnp.logical_not(used))
    def _():
        ys_ref[...] = jnp.zeros(ys_ref.shape, ys_ref.dtype)


def _experts(xs, tile_expert, n_used, layer, wg, wu, wd):
    n_rows, d = xs.shape
    tile = EXPERT_TILE
    n_tiles = n_rows // tile
    dff = wg.shape[3]

    def row_map(i, te, nu):
        return (jnp.minimum(i, nu[0] - 1), 0)

    def w_map(i, te, nu):
        return (layer, te[i], 0, 0)

    return pl.pallas_call(
        _expert_kernel,
        grid_spec=pltpu.PrefetchScalarGridSpec(
            num_scalar_prefetch=2,
            grid=(n_tiles,),
            in_specs=[pl.BlockSpec((tile, d), row_map),
                      pl.BlockSpec((1, 1, d, dff), w_map),
                      pl.BlockSpec((1, 1, d, dff), w_map),
                      pl.BlockSpec((1, 1, dff, d), w_map)],
            out_specs=pl.BlockSpec((tile, d), lambda i, te, nu: (i, 0)),
            scratch_shapes=[pltpu.VMEM((d, dff), bf16), pltpu.VMEM((d, dff), bf16),
                            pltpu.VMEM((dff, d), bf16)],
        ),
        out_shape=jax.ShapeDtypeStruct((n_rows, d), f32),
        compiler_params=_cparams(1),
        name="moe_experts",
    )(tile_expert, n_used, xs, wg, wu, wd)


def _combine_kernel(d1_ref, d2_ref, ys_hbm, x1_ref, gate_ref, lng_ref, lnb_ref, out_ref,
                    y1_buf, y2_buf, sems, *, tile, alpha):
    i = pl.program_id(0)
    slot = i % 2

    def row_copy(src, buf, slot_, j):
        return pltpu.make_async_copy(ys_hbm.at[pl.ds(src, 1)], buf.at[slot_, pl.ds(j, 1)],
                                     sems.at[slot_])

    def gather(tile_idx, slot_):
        base = tile_idx * tile

        def start(jq, carry):
            for u in range(DMA_UNROLL):
                j = _aligned(jq * DMA_UNROLL) + u
                row_copy(d1_ref[base + j], y1_buf, slot_, j).start()
                row_copy(d2_ref[base + j], y2_buf, slot_, j).start()
            return carry
        lax.fori_loop(0, tile // DMA_UNROLL, start, 0)

    pl.when(i == 0)(lambda: gather(0, 0))
    pl.when(i + 1 < pl.num_programs(0))(lambda: gather(i + 1, 1 - slot))

    def wait(jq, carry):
        for _ in range(DMA_UNROLL):
            row_copy(0, y1_buf, slot, 0).wait()
            row_copy(0, y2_buf, slot, 0).wait()
        return carry
    lax.fori_loop(0, tile // DMA_UNROLL, wait, 0)

    gates = gate_ref[...]
    moe = gates[:, 0:1] * y1_buf[slot] + gates[:, 1:2] * y2_buf[slot]
    out_ref[...] = _layer_norm(alpha * x1_ref[...] + moe, lng_ref[...], lnb_ref[...])


def _combine(ys, x1, d1, d2, gates, lng, lnb, alpha):
    t, d = x1.shape
    tile = COMBINE_TILE
    assert t % tile == 0
    tok = lambda i, a, b: (i, 0)
    const = lambda i, a, b: (0, 0)
    return pl.pallas_call(
        functools.partial(_combine_kernel, tile=tile, alpha=alpha),
        grid_spec=pltpu.PrefetchScalarGridSpec(
            num_scalar_prefetch=2,
            grid=(t // tile,),
            in_specs=[pl.BlockSpec(memory_space=pl.ANY),
                      pl.BlockSpec((tile, d), tok),
                      pl.BlockSpec((tile, 2), tok),
                      pl.BlockSpec((1, d), const),
                      pl.BlockSpec((1, d), const)],
            out_specs=pl.BlockSpec((tile, d), tok),
            scratch_shapes=[pltpu.VMEM((2, tile, d), f32), pltpu.VMEM((2, tile, d), f32),
                            pltpu.SemaphoreType.DMA((2,))],
        ),
        out_shape=jax.ShapeDtypeStruct((t, d), f32),
        compiler_params=_cparams(1),
        name="moe_combine",
    )(d1, d2, ys, x1, gates, lng, lnb)


def _moe(x1, ri, rf, cnt, layer, wg, wu, wd, lng, lnb, alpha):
    t, d = x1.shape
    tile = EXPERT_TILE
    n_tiles = -(-2 * t // tile) + N_EXPERTS
    counts = cnt[:, 0].astype(i32)
    padded = ((counts + tile - 1) // tile) * tile
    ends = jnp.cumsum(padded)
    starts = ends - padded
    n_used = (ends[-1] // tile).astype(i32).reshape(1)
    d1 = starts[ri[0]] + ri[2]
    d2 = starts[ri[1]] + ri[3]
    tile_start = jnp.minimum(jnp.arange(n_tiles, dtype=i32), n_used[0] - 1) * tile
    tile_expert = jnp.sum(tile_start[:, None] >= ends[None, :], axis=1).astype(i32)
    xs = _dispatch(x1, d1, d2, ends.astype(i32), n_used, n_tiles * tile)
    ys = _experts(xs, tile_expert, n_used, layer, wg, wu, wd)
    gates = jnp.transpose(rf[0:2])
    return _combine(ys, x1, d1, d2, gates, lng, lnb, alpha)


def _rw_proj_kernel(*refs, has_vres, n_ptiles):
    if has_vres:
        (x_ref, hp_ref, hs_ref, sh_ref, vf_ref, mix_ref, wr_ref, wk_ref, wv_ref, w0_ref, w1_ref,
         w2_ref, a0_ref, a1_ref, a2_ref, g1_ref, g2_ref, kk_ref, ka_ref, v0_ref, v1_ref, v2_ref,
         r_out, w_out, k_out, v_out, nkk_out, b_out, g_out, xp_buf) = refs
    else:
        (x_ref, hp_ref, hs_ref, sh_ref, mix_ref, wr_ref, wk_ref, wv_ref, w0_ref, w1_ref, w2_ref,
         a0_ref, a1_ref, a2_ref, g1_ref, g2_ref, kk_ref, ka_ref,
         r_out, w_out, k_out, v_out, nkk_out, b_out, g_out, xp_buf) = refs
    i = pl.program_id(0)
    tm = x_ref.shape[0]
    bp, bs = hp_ref.shape[0], hs_ref.shape[0]

    @pl.when(i < n_ptiles)
    def _():
        xp_buf[0:bp, :] = jnp.where(i == 0, 0.0, hp_ref[...])
        if tm > bp:
            xp_buf[bp:tm, :] = x_ref[0:tm - bp, :]

    @pl.when(i >= n_ptiles)
    def _():
        xp_buf[0:bs, :] = jnp.where(i == n_ptiles, sh_ref[...], hs_ref[...])
        if tm > bs:
            xp_buf[bs:tm, :] = x_ref[0:tm - bs, :]

    x = x_ref[...]
    xx = xp_buf[...] - x

    def mixed(i):
        return (x + xx * mix_ref[i:i + 1, :]).astype(bf16)

    def proj(wt_ref, xm):
        return lax.dot_general(wt_ref[...], xm, (((1,), (1,)), ((), ())),
                               preferred_element_type=f32)

    def wdot(wt_ref, y):
        return jnp.dot(wt_ref[...], y.astype(bf16), preferred_element_type=f32)

    def col(p_ref):
        return jnp.tile(p_ref[...], (1, tm // LANES))

    r_out[...] = proj(wr_ref, mixed(0))
    lora_w = jnp.tanh(proj(w1_ref, mixed(1)))
    w = -_softplus(-(col(w0_ref) + wdot(w2_ref, lora_w))) - 0.5
    w_out[...] = jnp.exp(-jnp.exp(w))
    k = proj(wk_ref, mixed(2))
    xv = mixed(3)
    v = proj(wv_ref, xv)
    a = jax.nn.sigmoid(col(a0_ref) + wdot(a2_ref, proj(a1_ref, mixed(4))))
    gate = jax.nn.sigmoid(proj(g1_ref, mixed(5)))
    g_out[...] = wdot(g2_ref, gate).astype(g_out.dtype)
    d = x.shape[1]
    kk = (k * col(kk_ref)).reshape(d // HEAD_DIM, HEAD_DIM, tm)
    norm = jnp.sqrt(jnp.sum(kk * kk, axis=1, keepdims=True))
    kk = kk / jnp.maximum(norm, 1e-12)
    nkk_out[...] = (-kk).reshape(d, tm)
    b_out[...] = (kk * a.reshape(kk.shape)).reshape(d, tm)
    k_out[...] = k * (1.0 + (a - 1.0) * col(ka_ref))
    if has_vres:
        lora_v = wdot(v2_ref, proj(v1_ref, xv))
        v = v + (vf_ref[...] - v) * jax.nn.sigmoid(col(v0_ref) + lora_v)
    v_out[...] = v


def _rw_proj(x, tp, bp, bs, shift_s, v_first, weights):
    t, d = x.shape
    tm = PROJ_TILE
    assert t % tm == 0 and tp % tm == 0 and tm % bp == 0 and tm % bs == 0
    n_ptiles = tp // tm
    has_vres = v_first is not None
    tile = lambda i: (i, 0)
    const = lambda i: (0, 0)
    halo_p = lambda i: (jnp.maximum(jnp.minimum(i, n_ptiles - 1) * (tm // bp) - 1, 0), 0)
    halo_s = lambda i: (jnp.maximum(i, n_ptiles) * (tm // bs) - 1, 0)
    cols = lambda i: (0, i)
    tok = [x, x, x, shift_s] + ([v_first] if has_vres else [])
    tok_specs = [pl.BlockSpec((tm, d), tile), pl.BlockSpec((bp, d), halo_p),
                 pl.BlockSpec((bs, d), halo_s), pl.BlockSpec((bs, d), const)]
    if has_vres:
        tok_specs.append(pl.BlockSpec((d, tm), cols))
    outs = pl.pallas_call(
        functools.partial(_rw_proj_kernel, has_vres=has_vres, n_ptiles=n_ptiles),
        grid=(t // tm,),
        in_specs=tok_specs + [pl.BlockSpec(w.shape, const) for w in weights],
        out_specs=[pl.BlockSpec((d, tm), cols) for _ in range(7)],
        out_shape=tuple(jax.ShapeDtypeStruct((d, t), f32) for _ in range(6))
        + (jax.ShapeDtypeStruct((d, t), bf16),),
        scratch_shapes=[pltpu.VMEM((tm, d), f32)],
        compiler_params=_cparams(1),
        name="rwkv_proj",
    )(*tok, *weights)
    return outs


def _block_transpose(src, dst, tmp):
    nb = src.shape[0]
    bw = LANES // nb
    blk = lax.broadcasted_iota(i32, tuple(src.shape[1:]), 1) // bw
    for i in range(nb):
        tmp[i] = pltpu.roll(src[i], i * bw, 1) if i else src[i]
    a, b = tmp, dst
    bit = 1
    while bit < nb:
        take = (blk & bit) != 0
        for i in range(nb):
            b[i] = jnp.where(take, a[(i + bit) % nb], a[i])
        a, b = b, a
        bit *= 2
    assert a is tmp
    for i in range(nb):
        dst[(nb - i) % nb] = pltpu.roll(tmp[i], i * bw, 1) if i else tmp[i]


def _wkv_kernel(r_ref, w_ref, k_ref, v_ref, nkk_ref, b_ref, s0_ref, lng_ref, lnb_ref, rk_ref,
                o_ref, s_ref, cur, o_buf, tmp_buf, *, steps, time_on_lanes):
    n = s_ref.shape[0]
    R, W, K, V, KK, A = range(6)
    srcs = (r_ref, w_ref, k_ref, v_ref, nkk_ref, b_ref)

    @pl.when(pl.program_id(1) == 0)
    def _():
        s_ref[...] = s0_ref[...]

    zeros = jnp.zeros((n, LANES), f32)

    def bcast(q, t, kidx):
        row = cur[q, t, pl.ds(kidx, 1), :]
        return jnp.broadcast_to(row, (n, LANES))

    def state_dot(t):
        def body(kq, sa):
            for u in range(SUBLANES):
                kidx = kq * SUBLANES + u
                sa = sa + s_ref[kidx] * bcast(KK, t, kidx)
            return sa
        return lax.fori_loop(0, n // SUBLANES, body, zeros)

    def advance(t, sa, look_ahead):
        vv = cur[V, t]

        def body(kq, carry):
            out, sa_next = carry
            for u in range(SUBLANES):
                kidx = kq * SUBLANES + u
                s_new = (s_ref[kidx] * bcast(W, t, kidx) + sa * bcast(A, t, kidx)
                         + vv * bcast(K, t, kidx))
                s_ref[kidx] = s_new
                out = out + s_new * bcast(R, t, kidx)
                if look_ahead:
                    sa_next = sa_next + s_new * bcast(KK, t + 1, kidx)
            return out, sa_next
        out, sa_next = lax.fori_loop(0, n // SUBLANES, body, (zeros, zeros))

        mu = jnp.mean(out, axis=0, keepdims=True)
        dlt = out - mu
        var = jnp.mean(dlt * dlt, axis=0, keepdims=True)
        o_n = dlt * lax.rsqrt(var + GN_EPS) * lng_ref[...] + lnb_ref[...]
        bonus = jnp.sum(cur[R, t] * cur[K, t] * rk_ref[...], axis=0, keepdims=True) * vv
        o_buf[t] = o_n + bonus
        return sa_next

    for q, ref in enumerate(srcs):
        if time_on_lanes:
            _block_transpose(ref, cur.at[q], tmp_buf)
        else:
            for t in range(steps):
                cur[q, t] = ref[0, :, t * LANES:(t + 1) * LANES]

    sa = state_dot(0)
    if steps > 1:
        sa = lax.fori_loop(0, steps - 1, lambda t, s: advance(t, s, True), sa)
    advance(steps - 1, sa, False)

    if time_on_lanes:
        _block_transpose(o_buf, o_ref, tmp_buf)
    else:
        for t in range(steps):
            o_ref[0, :, t * LANES:(t + 1) * LANES] = o_buf[t]


def _wkv_scan(streams, col0, n_steps, bn, s0, lng, lnb, rk):
    d = streams[0].shape[0]
    n = HEAD_DIM
    heads = d // n
    time_on_lanes = bn * heads == LANES
    if time_on_lanes:
        steps = LANES // bn
        assert n_steps % steps == 0 and col0 % LANES == 0
        grid = (1, n_steps // steps)
        blk = (heads, n, LANES)
        c0 = col0 // LANES
        in_map = lambda j, t: (0, 0, c0 + t)
        out_map = lambda j, t: (0, 0, t)
    else:
        steps = n_steps
        assert bn == LANES and col0 % (steps * LANES) == 0
        grid = (heads, 1)
        blk = (1, n, steps * LANES)
        c0 = col0 // (steps * LANES)
        in_map = lambda j, t: (j, 0, c0)
        out_map = lambda j, t: (j, 0, 0)
    per_lane = lambda j, t: (0, 0, j)
    par = lambda j, t: (0, j)
    views = [s.reshape(heads, n, s.shape[1]) for s in streams]
    o, s_fin = pl.pallas_call(
        functools.partial(_wkv_kernel, steps=steps, time_on_lanes=time_on_lanes),
        grid=grid,
        in_specs=([pl.BlockSpec(blk, in_map) for _ in range(6)]
                  + [pl.BlockSpec((n, n, LANES), per_lane)]
                  + [pl.BlockSpec((n, LANES), par) for _ in range(3)]),
        out_specs=[pl.BlockSpec(blk, out_map),
                   pl.BlockSpec((n, n, LANES), per_lane)],
        out_shape=(jax.ShapeDtypeStruct((heads, n, n_steps * bn), f32),
                   jax.ShapeDtypeStruct((n, n, heads * bn), f32)),
        scratch_shapes=[pltpu.VMEM((6, steps, n, LANES), f32), pltpu.VMEM((steps, n, LANES), f32),
                        pltpu.VMEM((steps, n, LANES), f32)],
        compiler_params=_cparams(2),
        name="wkv_scan",
    )(*views, s0, lng, lnb, rk)
    return o.reshape(d, n_steps * bn), s_fin


def _reorder_plan(shape_p, shape_s):
    bp, lp, d = shape_p
    bs, ls, _ = shape_s
    rows = bs * ls
    assert rows % bp == 0 and (bp * lp) % rows == 0
    return bp, bs, ls, d, rows, rows // bp, (bp * lp) // rows


def _to_time_major_kernel(xp_ref, xs_ref, o_ref, *, n_pblocks):
    bp, tb, _ = xp_ref.shape
    bs, ls, _ = xs_ref.shape

    @pl.when(pl.program_id(0) < n_pblocks)
    def _():
        for t in range(tb):
            o_ref[t * bp:(t + 1) * bp, :] = xp_ref[:, t, :]

    @pl.when(pl.program_id(0) >= n_pblocks)
    def _():
        for t in range(ls):
            o_ref[t * bs:(t + 1) * bs, :] = xs_ref[:, t, :]


def _to_time_major(x_prompt, x_sample):
    bp, bs, ls, d, rows, tb, n_pblocks = _reorder_plan(x_prompt.shape, x_sample.shape)
    return pl.pallas_call(
        functools.partial(_to_time_major_kernel, n_pblocks=n_pblocks),
        grid=(n_pblocks + 1,),
        in_specs=[pl.BlockSpec((bp, tb, d), lambda i: (0, jnp.minimum(i, n_pblocks - 1), 0)),
                  pl.BlockSpec((bs, ls, d), lambda i: (0, 0, 0))],
        out_specs=pl.BlockSpec((rows, d), lambda i: (i, 0)),
        out_shape=jax.ShapeDtypeStruct(((n_pblocks + 1) * rows, d), x_prompt.dtype),
        compiler_params=_cparams(1),
        name="to_time_major",
    )(x_prompt, x_sample)


def _to_batch_major_kernel(x_ref, yp_ref, ys_ref, *, n_pblocks):
    bp, tb, _ = yp_ref.shape
    bs, ls, _ = ys_ref.shape

    @pl.when(pl.program_id(0) < n_pblocks)
    def _():
        for t in range(tb):
            yp_ref[:, t, :] = x_ref[t * bp:(t + 1) * bp, :]

    @pl.when(pl.program_id(0) >= n_pblocks)
    def _():
        for t in range(ls):
            ys_ref[:, t, :] = x_ref[t * bs:(t + 1) * bs, :]


def _to_batch_major(x, shape_p, shape_s):
    bp, bs, ls, d, rows, tb, n_pblocks = _reorder_plan(shape_p, shape_s)
    return pl.pallas_call(
        functools.partial(_to_batch_major_kernel, n_pblocks=n_pblocks),
        grid=(n_pblocks + 1,),
        in_specs=[pl.BlockSpec((rows, d), lambda i: (i, 0))],
        out_specs=[pl.BlockSpec((bp, tb, d), lambda i: (0, jnp.minimum(i, n_pblocks - 1), 0)),
                   pl.BlockSpec((bs, ls, d), lambda i: (0, 0, 0))],
        out_shape=(jax.ShapeDtypeStruct(shape_p, x.dtype), jax.ShapeDtypeStruct(shape_s, x.dtype)),
        compiler_params=_cparams(1),
        name="to_batch_major",
    )(x)


def _lanes_param(p, bn):
    h = p.size // HEAD_DIM
    y = jnp.transpose(p.reshape(h, HEAD_DIM))
    return jnp.repeat(y, bn, axis=1)


def _col_param(p):
    return jnp.broadcast_to(p.reshape(-1, 1), (p.size, LANES))


def _pad_cols(w, mult=LANES):
    pad = (-w.shape[1]) % mult
    return jnp.pad(w, ((0, 0), (0, pad)))


def _pad_rows(w, mult=LANES):
    pad = (-w.shape[0]) % mult
    return jnp.pad(w, ((0, pad), (0, 0)))


def kernel(x_prompt, x_sample, state_s5_re, state_s5_im, state_wkv, state_shift, ln_gain, ln_bias,
           s5_lambda_re, s5_lambda_im, s5_log_step, s5_b_re, s5_b_im, s5_c_re, s5_c_im, s5_d,
           s5_w_out, s5_w_gate, rw_mix, rw_w_r, rw_w_k, rw_w_v, rw_w_o, rw_w0, rw_w1, rw_w2,
           rw_a0, rw_a1, rw_a2, rw_v0, rw_v1, rw_v2, rw_g1, rw_g2, rw_k_k, rw_k_a, rw_r_k,
           rw_lnx_g, rw_lnx_b, w_router, router_bias, moe_w_gate, moe_w_up, moe_w_down):
    bp, lp, d = x_prompt.shape
    bs, ls, _ = x_sample.shape
    tp, ts = bp * lp, bs * ls
    depth = ln_gain.shape[0]
    alpha = (2.0 * depth) ** 0.25
    n_groups = s5_lambda_re.shape[1]
    n_state = s5_lambda_re.shape[2]
    half = n_groups * n_state
    heads = d // HEAD_DIM

    x = _to_time_major(x_prompt, x_sample)

    wr_t = jnp.transpose(w_router)
    wr_hi = wr_t.astype(bf16)
    wr_lo = (wr_t - wr_hi.astype(f32)).astype(bf16)
    rbias = router_bias.reshape(N_EXPERTS, 1)
    tri = jnp.triu(jnp.ones((TOK_TILE, TOK_TILE), bf16), k=1)
    router = (wr_hi, wr_lo, rbias, tri)

    new_re_p, new_im_p, new_re_s, new_im_s = [], [], [], []
    new_wkv_p, new_wkv_s, new_shift_p, new_shift_s = [], [], [], []
    v_first = None
    groups_per_slab = MXU_DIM // S5_GROUP

    for i in range(depth):
        j = i // 2
        if i % 2 == 0:
            a_re, a_im, bb_re, bb_im = _s5_discretise(s5_lambda_re[j], s5_lambda_im[j], s5_log_step[j],
                                                      s5_b_re[j], s5_b_im[j])
            a_re8 = jnp.broadcast_to(a_re.reshape(1, half), (SUBLANES, half))
            a_im8 = jnp.broadcast_to(a_im.reshape(1, half), (SUBLANES, half))
            n_slab = n_groups // groups_per_slab

            def pack_b(bb):
                w = jnp.transpose(bb, (1, 0, 2)).reshape(n_slab, groups_per_slab, S5_GROUP, n_state)
                return _block_diag(w, groups_per_slab)
            wb = jnp.concatenate([pack_b(bb_re), pack_b(bb_im)], axis=2).astype(bf16)

            def pack_c(c):
                w = jnp.transpose(c, (0, 2, 1)).reshape(n_slab, groups_per_slab, n_state, S5_GROUP)
                return _block_diag(w, groups_per_slab).astype(bf16)
            wc_re, wc_im = pack_c(s5_c_re[j]), pack_c(s5_c_im[j])
            d_skip = s5_d[j].reshape(1, d)

            zero_h = jnp.zeros((bp, half), f32)
            z_p, hre_p, him_p = _s5_scan(x, 0, lp, bp, zero_h, zero_h, a_re8, a_im8,
                                         wb, wc_re, wc_im, d_skip)
            z_s, hre_s, him_s = _s5_scan(x, tp, ls, bs, state_s5_re[j].reshape(bs, half),
                                         state_s5_im[j].reshape(bs, half), a_re8, a_im8,
                                         wb, wc_re, wc_im, d_skip)
            new_re_p.append(hre_p.reshape(bp, n_groups, n_state))
            new_im_p.append(him_p.reshape(bp, n_groups, n_state))
            new_re_s.append(hre_s.reshape(bs, n_groups, n_state))
            new_im_s.append(him_s.reshape(bs, n_groups, n_state))
            x1, ri, rf, cnt = _post_mixer(
                _s5_out_kernel, "s5_out", z_p, z_s, [x],
                [s5_w_out[j].astype(bf16), s5_w_gate[j].astype(bf16)],
                ln_gain[i, 0].reshape(1, d), ln_bias[i, 0].reshape(1, d), router, alpha)
        else:
            new_shift_p.append(x[tp - bp:tp])
            new_shift_s.append(x[tp + ts - bs:tp + ts])
            wt = lambda w: jnp.transpose(w).astype(bf16)
            weights = [rw_mix[j],
                       wt(rw_w_r[j]), wt(rw_w_k[j]), wt(rw_w_v[j]),
                       _col_param(rw_w0[j]), wt(_pad_cols(rw_w1[j])), wt(_pad_rows(rw_w2[j])),
                       _col_param(rw_a0[j]), wt(_pad_cols(rw_a1[j])), wt(_pad_rows(rw_a2[j])),
                       wt(_pad_cols(rw_g1[j], MXU_DIM)), wt(_pad_rows(rw_g2[j], MXU_DIM)),
                       _col_param(rw_k_k[j]), _col_param(rw_k_a[j])]
            if j > 0:
                weights += [_col_param(rw_v0[j - 1]), wt(_pad_cols(rw_v1[j - 1])),
                            wt(_pad_rows(rw_v2[j - 1]))]
            r, w, k, v, nkk, b, g = _rw_proj(x, tp, bp, bs, state_shift[j],
                                            v_first if j > 0 else None, weights)
            if j == 0:
                v_first = v

            def run_group(col0, n_steps, bn, s0):
                o, s_fin = _wkv_scan((r, w, k, v, nkk, b), col0, n_steps, bn, s0,
                                     _lanes_param(rw_lnx_g[j], bn), _lanes_param(rw_lnx_b[j], bn),
                                     _lanes_param(rw_r_k[j], bn))
                s_fin = jnp.transpose(s_fin.reshape(HEAD_DIM, HEAD_DIM, heads, bn), (3, 2, 1, 0))
                return o, s_fin

            o_p, s_p = run_group(0, lp, bp, jnp.zeros((HEAD_DIM, HEAD_DIM, heads * bp), f32))
            s0_s = jnp.transpose(state_wkv[j], (3, 2, 1, 0)).reshape(HEAD_DIM, HEAD_DIM, heads * bs)
            o_s, s_s = run_group(tp, ls, bs, s0_s)
            new_wkv_p.append(s_p)
            new_wkv_s.append(s_s)
            x1, ri, rf, cnt = _post_mixer(
                _rw_out_kernel, "rwkv_out", o_p, o_s, [g, x], [wt(rw_w_o[j])],
                ln_gain[i, 0].reshape(1, d), ln_bias[i, 0].reshape(1, d), router, alpha,
                channel_major=True)

        x = _moe(x1, ri, rf, cnt, i, moe_w_gate, moe_w_up, moe_w_down,
                 ln_gain[i, 1].reshape(1, d), ln_bias[i, 1].reshape(1, d), alpha)

    y_prompt, y_sample = _to_batch_major(x, x_prompt.shape, x_sample.shape)
    return (y_prompt, y_sample,
            jnp.stack(new_re_p), jnp.stack(new_im_p), jnp.stack(new_wkv_p), jnp.stack(new_shift_p),
            jnp.stack(new_re_s), jnp.stack(new_im_s), jnp.stack(new_wkv_s), jnp.stack(new_shift_s))
```

```python
import functools
import math

import jax
import jax.numpy as jnp
from jax import lax
from jax.experimental import pallas as pl
from jax.experimental.pallas import tpu as pltpu

f32 = jnp.float32
bf16 = jnp.bfloat16
i32 = jnp.int32

S5_GROUP = 16
S5_STATE = 64
HEAD_DIM = 64
GN_EPS = 64e-5
LN_EPS = 1e-5
N_EXPERTS = 16
N_EXPERT_GROUPS = 4
EXPERTS_PER_GROUP = N_EXPERTS // N_EXPERT_GROUPS

SUBLANES = 8
LANES = 128
MXU_DIM = 256
VMEM_LIMIT = 56 * 1024 * 1024

TOK_TILE = 512
PROJ_TILE = 256
S5_ROWS = 256
EXPERT_TILE = 512
COMBINE_TILE = 256
DISPATCH_TILE = 512
DMA_UNROLL = 8


def _cparams(n_axes=1, **kw):
    return pltpu.CompilerParams(dimension_semantics=("arbitrary",) * n_axes,
                                vmem_limit_bytes=VMEM_LIMIT, **kw)


def _bdot(a, w):
    return jnp.dot(a.astype(bf16), w, preferred_element_type=f32)


def _layer_norm(y, g, b):
    mu = jnp.mean(y, axis=-1, keepdims=True)
    d = y - mu
    var = jnp.mean(d * d, axis=-1, keepdims=True)
    return d * lax.rsqrt(var + LN_EPS) * g + b


def _aligned(idx):
    return idx if isinstance(idx, int) else pl.multiple_of(idx, SUBLANES)


def _softplus(x):
    return jnp.maximum(x, 0.0) + jnp.log1p(jnp.exp(-jnp.abs(x)))


def _route(x1, wr_hi, wr_lo, rbias, tri, cnt_ref, ri_ref, rf_ref, cnt_out_ref):
    tm = x1.shape[0]
    xh = x1.astype(bf16)
    xl = (x1 - xh.astype(f32)).astype(bf16)
    dn = (((1,), (1,)), ((), ()))
    logits = (lax.dot_general(wr_hi, xh, dn, preferred_element_type=f32)
              + lax.dot_general(wr_lo, xh, dn, preferred_element_type=f32)
              + lax.dot_general(wr_hi, xl, dn, preferred_element_type=f32))
    m = jnp.max(logits, axis=0, keepdims=True)
    ex = jnp.exp(logits - m)
    probs = ex / jnp.sum(ex, axis=0, keepdims=True)
    sel = probs + rbias
    s = [sel[e:e + 1, :] for e in range(N_EXPERTS)]
    p = [probs[e:e + 1, :] for e in range(N_EXPERTS)]

    best = None
    gi = None
    for g in range(N_EXPERT_GROUPS):
        a, b, c, d = s[4 * g:4 * g + 4]
        hi1, lo1 = jnp.maximum(a, b), jnp.minimum(a, b)
        hi2, lo2 = jnp.maximum(c, d), jnp.minimum(c, d)
        score = jnp.maximum(hi1, hi2) + jnp.maximum(jnp.minimum(hi1, hi2), jnp.maximum(lo1, lo2))
        if g == 0:
            best, gi = score, jnp.zeros(score.shape, i32)
        else:
            better = score > best
            best = jnp.where(better, score, best)
            gi = jnp.where(better, g, gi)

    def in_group(rows, j):
        out = rows[12 + j]
        for g in (2, 1, 0):
            out = jnp.where(gi == g, rows[4 * g + j], out)
        return out

    v = [in_group(s, j) for j in range(EXPERTS_PER_GROUP)]
    pv = [in_group(p, j) for j in range(EXPERTS_PER_GROUP)]
    order = []
    for j in range(EXPERTS_PER_GROUP):
        r = jnp.zeros(v[j].shape, i32)
        for i in range(EXPERTS_PER_GROUP):
            if i < j:
                r = r + jnp.where(v[i] >= v[j], 1, 0)
            elif i > j:
                r = r + jnp.where(v[i] > v[j], 1, 0)
        order.append(r)
    j1 = sum(jnp.where(order[j] == 0, j, 0) for j in range(EXPERTS_PER_GROUP))
    j2 = sum(jnp.where(order[j] == 1, j, 0) for j in range(EXPERTS_PER_GROUP))
    p1 = sum(jnp.where(order[j] == 0, pv[j], 0.0) for j in range(EXPERTS_PER_GROUP))
    p2 = sum(jnp.where(order[j] == 1, pv[j], 0.0) for j in range(EXPERTS_PER_GROUP))
    e1 = gi * EXPERTS_PER_GROUP + j1
    e2 = gi * EXPERTS_PER_GROUP + j2
    den = p1 + p2
    g1 = p1 / den
    g2 = p2 / den

    eio = lax.broadcasted_iota(i32, (N_EXPERTS, tm), 0)
    oh1 = eio == e1
    oh2 = eio == e2
    oh = jnp.where(oh1, 1.0, 0.0) + jnp.where(oh2, 1.0, 0.0)
    before = jnp.dot(oh.astype(bf16), tri, preferred_element_type=f32)
    before = before + cnt_ref[:, 0:1]
    r1 = jnp.sum(jnp.where(oh1, before, 0.0), axis=0, keepdims=True)
    r2 = jnp.sum(jnp.where(oh2, before, 0.0), axis=0, keepdims=True)
    cnt_new = cnt_ref[...] + jnp.sum(oh, axis=1, keepdims=True)
    cnt_ref[...] = cnt_new
    cnt_out_ref[...] = cnt_new

    ri_ref[...] = jnp.zeros(ri_ref.shape, i32)
    rf_ref[...] = jnp.zeros(rf_ref.shape, f32)
    ri_ref[0:1, :] = e1
    ri_ref[1:2, :] = e2
    ri_ref[2:3, :] = r1.astype(i32)
    ri_ref[3:4, :] = r2.astype(i32)
    rf_ref[0:1, :] = g1
    rf_ref[1:2, :] = g2


def _s5_disc_kernel(lre_ref, lim_ref, lstep_ref, bre_ref, bim_ref,
                    are_ref, aim_ref, bbre_ref, bbim_ref):
    lre = lre_ref[...]
    lim = lim_ref[...]
    step = jnp.exp(lstep_ref[...])
    mag = jnp.exp(lre * step)
    th = lim * step
    a_re = mag * jnp.cos(th)
    a_im = mag * jnp.sin(th)
    den = lre * lre + lim * lim
    q_re = ((a_re - 1.0) * lre + a_im * lim) / den
    q_im = (a_im * lre - (a_re - 1.0) * lim) / den
    are_ref[...] = a_re
    aim_ref[...] = a_im
    b_re = bre_ref[...]
    b_im = bim_ref[...]
    bbre_ref[...] = q_re[None] * b_re - q_im[None] * b_im
    bbim_ref[...] = q_re[None] * b_im + q_im[None] * b_re


def _s5_discretise(lam_re, lam_im, log_step, b_re, b_im):
    g, p = lam_re.shape
    c = b_re.shape[-1]
    outs = pl.pallas_call(
        _s5_disc_kernel,
        out_shape=(jax.ShapeDtypeStruct((g, p), f32), jax.ShapeDtypeStruct((g, p), f32),
                   jax.ShapeDtypeStruct((c, g, p), f32), jax.ShapeDtypeStruct((c, g, p), f32)),
        name="s5_discretise",
    )(lam_re, lam_im, log_step.reshape(g, 1),
      jnp.transpose(b_re, (2, 0, 1)), jnp.transpose(b_im, (2, 0, 1)))
    return outs


def _block_diag(w, n_blk):
    s, _, a, b = w.shape
    eye = jnp.eye(n_blk, dtype=w.dtype)
    out = w[:, :, :, None, :] * eye[None, :, None, :, None]
    return out.reshape(s, n_blk * a, n_blk * b)


def _s5_kernel(u_ref, h0re_ref, h0im_ref, are_ref, aim_ref, wb_ref, wcre_ref, wcim_ref, d_ref,
               z_ref, hre_ref, him_ref, bu_ref, *, bn, tb):
    i = pl.program_id(0)
    half = are_ref.shape[1]
    n_slab = wb_ref.shape[0]
    kw = wb_ref.shape[1]
    sw = half // n_slab
    cw = 1024

    @pl.when(i == 0)
    def _():
        hre_ref[...] = h0re_ref[...]
        him_ref[...] = h0im_ref[...]

    ub = u_ref[...].astype(bf16)
    for s in range(n_slab):
        res = jnp.dot(ub[:, s * kw:(s + 1) * kw], wb_ref[s], preferred_element_type=f32)
        bu_ref[:, s * sw:(s + 1) * sw] = res[:, :sw]
        bu_ref[:, half + s * sw:half + (s + 1) * sw] = res[:, sw:]

    def advance(t, first):
        def sub(j, carry):
            rr = _aligned(t * bn + j * SUBLANES)
            for c0 in range(0, half, cw):
                if first:
                    jr = _aligned(j * SUBLANES)
                    p_re = hre_ref[pl.ds(jr, SUBLANES), c0:c0 + cw]
                    p_im = him_ref[pl.ds(jr, SUBLANES), c0:c0 + cw]
                else:
                    pr = _aligned(rr - bn)
                    p_re = bu_ref[pl.ds(pr, SUBLANES), c0:c0 + cw]
                    p_im = bu_ref[pl.ds(pr, SUBLANES), half + c0:half + c0 + cw]
                a_re = are_ref[:, c0:c0 + cw]
                a_im = aim_ref[:, c0:c0 + cw]
                n_re = a_re * p_re - a_im * p_im + bu_ref[pl.ds(rr, SUBLANES), c0:c0 + cw]
                n_im = a_re * p_im + a_im * p_re + bu_ref[pl.ds(rr, SUBLANES), half + c0:half + c0 + cw]
                bu_ref[pl.ds(rr, SUBLANES), c0:c0 + cw] = n_re
                bu_ref[pl.ds(rr, SUBLANES), half + c0:half + c0 + cw] = n_im
            return carry
        if bn == SUBLANES:
            sub(0, 0)
        else:
            lax.fori_loop(0, bn // SUBLANES, sub, 0)

    advance(0, True)
    if tb > 1:
        def body(t, carry):
            advance(t, False)
            return carry
        lax.fori_loop(1, tb, body, 0)
    hre_ref[...] = bu_ref[(tb - 1) * bn:tb * bn, 0:half]
    him_ref[...] = bu_ref[(tb - 1) * bn:tb * bn, half:2 * half]

    nw = wcre_ref.shape[2]
    for n in range(wcre_ref.shape[0]):
        h_re = bu_ref[:, n * sw:(n + 1) * sw]
        h_im = bu_ref[:, half + n * sw:half + (n + 1) * sw]
        y = _bdot(h_re, wcre_ref[n]) - _bdot(h_im, wcim_ref[n])
        y = y + d_ref[:, n * nw:(n + 1) * nw] * u_ref[:, n * nw:(n + 1) * nw]
        z_ref[:, n * nw:(n + 1) * nw] = jax.nn.gelu(y, approximate=True).astype(z_ref.dtype)


def _s5_scan(x, row0, n_steps, bn, h0_re, h0_im, a_re8, a_im8, wb, wc_re, wc_im, d_skip):
    d = x.shape[1]
    tb = max(1, S5_ROWS // bn)
    rows = tb * bn
    assert n_steps % tb == 0 and row0 % rows == 0
    half = a_re8.shape[1]
    blk0 = row0 // rows
    const2 = lambda i: (0, 0)
    const3 = lambda i: (0, 0, 0)
    z, h_re, h_im = pl.pallas_call(
        functools.partial(_s5_kernel, bn=bn, tb=tb),
        grid=(n_steps // tb,),
        in_specs=[
            pl.BlockSpec((rows, d), lambda i: (i + blk0, 0)),
            pl.BlockSpec((bn, half), const2),
            pl.BlockSpec((bn, half), const2),
            pl.BlockSpec(a_re8.shape, const2),
            pl.BlockSpec(a_im8.shape, const2),
            pl.BlockSpec(wb.shape, const3),
            pl.BlockSpec(wc_re.shape, const3),
            pl.BlockSpec(wc_im.shape, const3),
            pl.BlockSpec((1, d), const2),
        ],
        out_specs=[
            pl.BlockSpec((rows, d), lambda i: (i, 0)),
            pl.BlockSpec((bn, half), const2),
            pl.BlockSpec((bn, half), const2),
        ],
        out_shape=(jax.ShapeDtypeStruct((n_steps * bn, d), bf16),
                   jax.ShapeDtypeStruct((bn, half), f32),
                   jax.ShapeDtypeStruct((bn, half), f32)),
        scratch_shapes=[pltpu.VMEM((rows, 2 * half), f32)],
        compiler_params=_cparams(1),
        name="s5_scan",
    )(x, h0_re, h0_im, a_re8, a_im8, wb, wc_re, wc_im, d_skip)
    return z, h_re, h_im


def _s5_out_kernel(zp_ref, zs_ref, x_ref, wo_ref, wg_ref, lng_ref, lnb_ref,
                   wrh_ref, wrl_ref, rb_ref, tri_ref,
                   x1_ref, ri_ref, rf_ref, cnt_out_ref, cnt_ref, *, alpha, n_ptiles):
    @pl.when(pl.program_id(0) == 0)
    def _():
        cnt_ref[...] = jnp.zeros(cnt_ref.shape, f32)

    z = jnp.where(pl.program_id(0) < n_ptiles, zp_ref[...], zs_ref[...])
    mixed = (jnp.dot(z, wo_ref[...], preferred_element_type=f32)
             * jax.nn.sigmoid(jnp.dot(z, wg_ref[...], preferred_element_type=f32)))
    x1 = _layer_norm(alpha * x_ref[...] + mixed, lng_ref[...], lnb_ref[...])
    x1_ref[...] = x1
    _route(x1, wrh_ref[...], wrl_ref[...], rb_ref[...], tri_ref[...], cnt_ref, ri_ref, rf_ref,
           cnt_out_ref)


def _rw_out_kernel(op_ref, os_ref, g_ref, x_ref, wo_ref, lng_ref, lnb_ref,
                   wrh_ref, wrl_ref, rb_ref, tri_ref,
                   x1_ref, ri_ref, rf_ref, cnt_out_ref, cnt_ref, *, alpha, n_ptiles):
    @pl.when(pl.program_id(0) == 0)
    def _():
        cnt_ref[...] = jnp.zeros(cnt_ref.shape, f32)

    o = jnp.where(pl.program_id(0) < n_ptiles, op_ref[...], os_ref[...])
    mixed_t = jnp.dot(wo_ref[...], (o * g_ref[...]).astype(bf16), preferred_element_type=f32)
    mixed = jnp.transpose(mixed_t)
    x1 = _layer_norm(alpha * x_ref[...] + mixed, lng_ref[...], lnb_ref[...])
    x1_ref[...] = x1
    _route(x1, wrh_ref[...], wrl_ref[...], rb_ref[...], tri_ref[...], cnt_ref, ri_ref, rf_ref,
           cnt_out_ref)


def _post_mixer(kernel_fn, name, mix_p, mix_s, tok_inputs, weights, lng, lnb, router, alpha,
                channel_major=False):
    t, d = tok_inputs[-1].shape
    tm = TOK_TILE
    tok_axis = 1 if channel_major else 0
    assert t % tm == 0 and mix_p.shape[tok_axis] % tm == 0 and mix_s.shape[tok_axis] % tm == 0
    n_ptiles = mix_p.shape[tok_axis] // tm
    wr_hi, wr_lo, rbias, tri = router
    tile = lambda i: (i, 0)
    const = lambda i: (0, 0)
    if channel_major:
        blk = (d, tm)
        at = lambda f: (lambda i: (0, f(i)))
    else:
        blk = (tm, d)
        at = lambda f: (lambda i: (f(i), 0))
    in_specs = ([pl.BlockSpec(blk, at(lambda i: jnp.minimum(i, n_ptiles - 1))),
                 pl.BlockSpec(blk, at(lambda i: jnp.maximum(i - n_ptiles, 0)))]
                + [pl.BlockSpec(blk, at(lambda i: i)) for _ in tok_inputs[:-1]]
                + [pl.BlockSpec((tm, d), tile)]
                + [pl.BlockSpec(w.shape, const) for w in weights]
                + [pl.BlockSpec((1, d), const), pl.BlockSpec((1, d), const),
                   pl.BlockSpec(wr_hi.shape, const), pl.BlockSpec(wr_lo.shape, const),
                   pl.BlockSpec(rbias.shape, const), pl.BlockSpec(tri.shape, const)])
    x1, ri, rf, cnt = pl.pallas_call(
        functools.partial(kernel_fn, alpha=alpha, n_ptiles=n_ptiles),
        grid=(t // tm,),
        in_specs=in_specs,
        out_specs=[pl.BlockSpec((tm, d), tile),
                   pl.BlockSpec((SUBLANES, tm), lambda i: (0, i)),
                   pl.BlockSpec((SUBLANES, tm), lambda i: (0, i)),
                   pl.BlockSpec((N_EXPERTS, LANES), const)],
        out_shape=(jax.ShapeDtypeStruct((t, d), f32),
                   jax.ShapeDtypeStruct((SUBLANES, t), i32),
                   jax.ShapeDtypeStruct((SUBLANES, t), f32),
                   jax.ShapeDtypeStruct((N_EXPERTS, LANES), f32)),
        scratch_shapes=[pltpu.VMEM((N_EXPERTS, LANES), f32)],
        compiler_params=_cparams(1),
        name=name,
    )(mix_p, mix_s, *tok_inputs, *weights, lng, lnb, wr_hi, wr_lo, rbias, tri)
    return x1, ri, rf, cnt


def _dispatch_kernel(d1_ref, d2_ref, ends_ref, nu_ref, x_ref, xs_hbm, zero_buf, sem, zsem, *,
                     tile, ztile, n_ztiles):
    base = pl.program_id(0) * tile

    @pl.when(pl.program_id(0) == 0)
    def _():
        zero_buf[...] = jnp.zeros(zero_buf.shape, zero_buf.dtype)

        def zero_tile(idx):
            return pltpu.make_async_copy(zero_buf, xs_hbm.at[pl.ds(idx * ztile, ztile)], zsem)

        def each(fn):
            for e in range(N_EXPERTS):
                begin = ends_ref[e - 1] if e else 0
                pl.when(ends_ref[e] > begin)(lambda e=e: fn(ends_ref[e] // ztile - 1))

            def tail(idx, carry):
                pl.when(idx >= nu_ref[0])(lambda: fn(idx))
                return carry
            lax.fori_loop(0, n_ztiles, tail, 0)
        each(lambda idx: zero_tile(idx).start())
        each(lambda idx: zero_tile(idx).wait())

    def row_copy(j, dst):
        return pltpu.make_async_copy(x_ref.at[pl.ds(j, 1)], xs_hbm.at[pl.ds(dst, 1)], sem)

    for j in range(tile):
        row_copy(j, d1_ref[base + j]).start(priority=0)
        row_copy(j, d2_ref[base + j]).start(priority=1)

    def wait(jq, carry):
        for _ in range(2 * DMA_UNROLL):
            row_copy(0, 0).wait()
        return carry
    lax.fori_loop(0, tile // DMA_UNROLL, wait, 0)


def _dispatch(x1, d1, d2, ends, n_used, n_rows):
    t, d = x1.shape
    tile = DISPATCH_TILE
    ztile = EXPERT_TILE
    assert t % tile == 0 and n_rows % ztile == 0
    return pl.pallas_call(
        functools.partial(_dispatch_kernel, tile=tile, ztile=ztile, n_ztiles=n_rows // ztile),
        grid_spec=pltpu.PrefetchScalarGridSpec(
            num_scalar_prefetch=4,
            grid=(t // tile,),
            in_specs=[pl.BlockSpec((tile, d), lambda i, *_: (i, 0))],
            out_specs=pl.BlockSpec(memory_space=pl.ANY),
            scratch_shapes=[pltpu.VMEM((ztile, d), x1.dtype),
                            pltpu.SemaphoreType.DMA(()), pltpu.SemaphoreType.DMA(())],
        ),
        out_shape=jax.ShapeDtypeStruct((n_rows, d), x1.dtype),
        compiler_params=_cparams(1, has_side_effects=True),
        name="moe_dispatch",
    )(d1, d2, ends, n_used, x1)


def _expert_kernel(te_ref, nu_ref, xs_ref, wg_ref, wu_ref, wd_ref, ys_ref, wgb, wub, wdb):
    i = pl.program_id(0)
    used = i < nu_ref[0]
    prev = te_ref[jnp.maximum(i - 1, 0)]
    fresh = jnp.logical_or(i == 0, te_ref[i] != prev)

    @pl.when(jnp.logical_and(used, fresh))
    def _():
        wgb[...] = wg_ref[0, 0].astype(bf16)
        wub[...] = wu_ref[0, 0].astype(bf16)
        wdb[...] = wd_ref[0, 0].astype(bf16)

    @pl.when(used)
    def _():
        x = xs_ref[...].astype(bf16)
        h = (jax.nn.silu(jnp.dot(x, wgb[...], preferred_element_type=f32))
             * jnp.dot(x, wub[...], preferred_element_type=f32))
        ys_ref[...] = jnp.dot(h.astype(bf16), wdb[...], preferred_element_type=f32)

    @pl.when(jnp.logical_not(used))
    def _():
        ys_ref[...] = jnp.zeros(ys_ref.shape, ys_ref.dtype)


def _experts(xs, tile_expert, n_used, layer, wg, wu, wd):
    n_rows, d = xs.shape
    tile = EXPERT_TILE
    n_tiles = n_rows // tile
    dff = wg.shape[3]

    def row_map(i, te, nu):
        return (jnp.minimum(i, nu[0] - 1), 0)

    def w_map(i, te, nu):
        return (layer, te[i], 0, 0)

    return pl.pallas_call(
        _expert_kernel,
        grid_spec=pltpu.PrefetchScalarGridSpec(
            num_scalar_prefetch=2,
            grid=(n_tiles,),
            in_specs=[pl.BlockSpec((tile, d), row_map),
                      pl.BlockSpec((1, 1, d, dff), w_map),
                      pl.BlockSpec((1, 1, d, dff), w_map),
                      pl.BlockSpec((1, 1, dff, d), w_map)],
            out_specs=pl.BlockSpec((tile, d), lambda i, te, nu: (i, 0)),
            scratch_shapes=[pltpu.VMEM((d, dff), bf16), pltpu.VMEM((d, dff), bf16),
                            pltpu.VMEM((dff, d), bf16)],
        ),
        out_shape=jax.ShapeDtypeStruct((n_rows, d), f32),
        compiler_params=_cparams(1),
        name="moe_experts",
    )(tile_expert, n_used, xs, wg, wu, wd)


def _combine_kernel(d1_ref, d2_ref, ys_hbm, x1_ref, gate_ref, lng_ref, lnb_ref, out_ref,
                    y1_buf, y2_buf, sems, *, tile, alpha):
    i = pl.program_id(0)
    slot = i % 2

    def row_copy(src, buf, slot_, j):
        return pltpu.make_async_copy(ys_hbm.at[pl.ds(src, 1)], buf.at[slot_, pl.ds(j, 1)],
                                     sems.at[slot_])

    def gather(tile_idx, slot_):
        base = tile_idx * tile

        for j in range(tile):
            row_copy(d1_ref[base + j], y1_buf, slot_, j).start(priority=0)
            row_copy(d2_ref[base + j], y2_buf, slot_, j).start(priority=1)

    pl.when(i == 0)(lambda: gather(0, 0))
    pl.when(i + 1 < pl.num_programs(0))(lambda: gather(i + 1, 1 - slot))

    def wait(jq, carry):
        for _ in range(DMA_UNROLL):
            row_copy(0, y1_buf, slot, 0).wait()
            row_copy(0, y2_buf, slot, 0).wait()
        return carry
    lax.fori_loop(0, tile // DMA_UNROLL, wait, 0)

    gates = gate_ref[...]
    moe = gates[:, 0:1] * y1_buf[slot] + gates[:, 1:2] * y2_buf[slot]
    out_ref[...] = _layer_norm(alpha * x1_ref[...] + moe, lng_ref[...], lnb_ref[...])


def _combine(ys, x1, d1, d2, gates, lng, lnb, alpha):
    t, d = x1.shape
    tile = COMBINE_TILE
    assert t % tile == 0
    tok = lambda i, a, b: (i, 0)
    const = lambda i, a, b: (0, 0)
    return pl.pallas_call(
        functools.partial(_combine_kernel, tile=tile, alpha=alpha),
        grid_spec=pltpu.PrefetchScalarGridSpec(
            num_scalar_prefetch=2,
            grid=(t // tile,),
            in_specs=[pl.BlockSpec(memory_space=pl.ANY),
                      pl.BlockSpec((tile, d), tok),
                      pl.BlockSpec((tile, 2), tok),
                      pl.BlockSpec((1, d), const),
                      pl.BlockSpec((1, d), const)],
            out_specs=pl.BlockSpec((tile, d), tok),
            scratch_shapes=[pltpu.VMEM((2, tile, d), f32), pltpu.VMEM((2, tile, d), f32),
                            pltpu.SemaphoreType.DMA((2,))],
        ),
        out_shape=jax.ShapeDtypeStruct((t, d), f32),
        compiler_params=_cparams(1),
        name="moe_combine",
    )(d1, d2, ys, x1, gates, lng, lnb)


def _moe(x1, ri, rf, cnt, layer, wg, wu, wd, lng, lnb, alpha):
    t, d = x1.shape
    tile = EXPERT_TILE
    n_tiles = -(-2 * t // tile) + N_EXPERTS
    counts = cnt[:, 0].astype(i32)
    padded = ((counts + tile - 1) // tile) * tile
    ends = jnp.cumsum(padded)
    starts = ends - padded
    n_used = (ends[-1] // tile).astype(i32).reshape(1)
    d1 = starts[ri[0]] + ri[2]
    d2 = starts[ri[1]] + ri[3]
    tile_start = jnp.minimum(jnp.arange(n_tiles, dtype=i32), n_used[0] - 1) * tile
    tile_expert = jnp.sum(tile_start[:, None] >= ends[None, :], axis=1).astype(i32)
    xs = _dispatch(x1, d1, d2, ends.astype(i32), n_used, n_tiles * tile)
    ys = _experts(xs, tile_expert, n_used, layer, wg, wu, wd)
    gates = jnp.transpose(rf[0:2])
    return _combine(ys, x1, d1, d2, gates, lng, lnb, alpha)


def _rw_proj_kernel(*refs, has_vres, n_ptiles):
    if has_vres:
        (x_ref, hp_ref, hs_ref, sh_ref, vf_ref, mix_ref, wr_ref, wk_ref, wv_ref, w0_ref, w1_ref,
         w2_ref, a0_ref, a1_ref, a2_ref, g1_ref, g2_ref, kk_ref, ka_ref, v0_ref, v1_ref, v2_ref,
         r_out, w_out, k_out, v_out, nkk_out, b_out, g_out, xp_buf) = refs
    else:
        (x_ref, hp_ref, hs_ref, sh_ref, mix_ref, wr_ref, wk_ref, wv_ref, w0_ref, w1_ref, w2_ref,
         a0_ref, a1_ref, a2_ref, g1_ref, g2_ref, kk_ref, ka_ref,
         r_out, w_out, k_out, v_out, nkk_out, b_out, g_out, xp_buf) = refs
    i = pl.program_id(0)
    tm = x_ref.shape[0]
    bp, bs = hp_ref.shape[0], hs_ref.shape[0]

    @pl.when(i < n_ptiles)
    def _():
        xp_buf[0:bp, :] = jnp.where(i == 0, 0.0, hp_ref[...])
        if tm > bp:
            xp_buf[bp:tm, :] = x_ref[0:tm - bp, :]

    @pl.when(i >= n_ptiles)
    def _():
        xp_buf[0:bs, :] = jnp.where(i == n_ptiles, sh_ref[...], hs_ref[...])
        if tm > bs:
            xp_buf[bs:tm, :] = x_ref[0:tm - bs, :]

    x = x_ref[...]
    xx = xp_buf[...] - x

    def mixed(i):
        return (x + xx * mix_ref[i:i + 1, :]).astype(bf16)

    def proj(wt_ref, xm):
        return lax.dot_general(wt_ref[...], xm, (((1,), (1,)), ((), ())),
                               preferred_element_type=f32)

    def wdot(wt_ref, y):
        return jnp.dot(wt_ref[...], y.astype(bf16), preferred_element_type=f32)

    def col(p_ref):
        return jnp.tile(p_ref[...], (1, tm // LANES))

    r_out[...] = proj(wr_ref, mixed(0))
    lora_w = jnp.tanh(proj(w1_ref, mixed(1)))
    w = -_softplus(-(col(w0_ref) + wdot(w2_ref, lora_w))) - 0.5
    w_out[...] = jnp.exp(-jnp.exp(w))
    k = proj(wk_ref, mixed(2))
    xv = mixed(3)
    v = proj(wv_ref, xv)
    a = jax.nn.sigmoid(col(a0_ref) + wdot(a2_ref, proj(a1_ref, mixed(4))))
    gate = jax.nn.sigmoid(proj(g1_ref, mixed(5)))
    g_out[...] = wdot(g2_ref, gate).astype(g_out.dtype)
    d = x.shape[1]
    kk = (k * col(kk_ref)).reshape(d // HEAD_DIM, HEAD_DIM, tm)
    norm = jnp.sqrt(jnp.sum(kk * kk, axis=1, keepdims=True))
    kk = kk / jnp.maximum(norm, 1e-12)
    nkk_out[...] = (-kk).reshape(d, tm)
    b_out[...] = (kk * a.reshape(kk.shape)).reshape(d, tm)
    k_out[...] = k * (1.0 + (a - 1.0) * col(ka_ref))
    if has_vres:
        lora_v = wdot(v2_ref, proj(v1_ref, xv))
        v = v + (vf_ref[...] - v) * jax.nn.sigmoid(col(v0_ref) + lora_v)
    v_out[...] = v


def _rw_proj(x, tp, bp, bs, shift_s, v_first, weights):
    t, d = x.shape
    tm = PROJ_TILE
    assert t % tm == 0 and tp % tm == 0 and tm % bp == 0 and tm % bs == 0
    n_ptiles = tp // tm
    has_vres = v_first is not None
    tile = lambda i: (i, 0)
    const = lambda i: (0, 0)
    halo_p = lambda i: (jnp.maximum(jnp.minimum(i, n_ptiles - 1) * (tm // bp) - 1, 0), 0)
    halo_s = lambda i: (jnp.maximum(i, n_ptiles) * (tm // bs) - 1, 0)
    cols = lambda i: (0, i)
    tok = [x, x, x, shift_s] + ([v_first] if has_vres else [])
    tok_specs = [pl.BlockSpec((tm, d), tile), pl.BlockSpec((bp, d), halo_p),
                 pl.BlockSpec((bs, d), halo_s), pl.BlockSpec((bs, d), const)]
    if has_vres:
        tok_specs.append(pl.BlockSpec((d, tm), cols))
    outs = pl.pallas_call(
        functools.partial(_rw_proj_kernel, has_vres=has_vres, n_ptiles=n_ptiles),
        grid=(t // tm,),
        in_specs=tok_specs + [pl.BlockSpec(w.shape, const) for w in weights],
        out_specs=[pl.BlockSpec((d, tm), cols) for _ in range(7)],
        out_shape=tuple(jax.ShapeDtypeStruct((d, t), f32) for _ in range(6))
        + (jax.ShapeDtypeStruct((d, t), bf16),),
        scratch_shapes=[pltpu.VMEM((tm, d), f32)],
        compiler_params=_cparams(1),
        name="rwkv_proj",
    )(*tok, *weights)
    return outs


def _block_transpose(src, dst, tmp):
    nb = src.shape[0]
    bw = LANES // nb
    blk = lax.broadcasted_iota(i32, tuple(src.shape[1:]), 1) // bw
    for i in range(nb):
        tmp[i] = pltpu.roll(src[i], i * bw, 1) if i else src[i]
    a, b = tmp, dst
    bit = 1
    while bit < nb:
        take = (blk & bit) != 0
        for i in range(nb):
            b[i] = jnp.where(take, a[(i + bit) % nb], a[i])
        a, b = b, a
        bit *= 2
    assert a is tmp
    for i in range(nb):
        dst[(nb - i) % nb] = pltpu.roll(tmp[i], i * bw, 1) if i else tmp[i]


def _wkv_kernel(r_ref, w_ref, k_ref, v_ref, nkk_ref, b_ref, s0_ref, lng_ref, lnb_ref, rk_ref,
                o_ref, s_ref, cur, o_buf, tmp_buf, *, steps, time_on_lanes):
    n = s_ref.shape[0]
    R, W, K, V, KK, A = range(6)
    srcs = (r_ref, w_ref, k_ref, v_ref, nkk_ref, b_ref)

    @pl.when(pl.program_id(1) == 0)
    def _():
        s_ref[...] = s0_ref[...]

    zeros = jnp.zeros((n, LANES), f32)

    def bcast(q, t, kidx):
        row = cur[q, t, pl.ds(kidx, 1), :]
        return jnp.broadcast_to(row, (n, LANES))

    def state_dot(t):
        def body(kq, sa):
            for u in range(SUBLANES):
                kidx = kq * SUBLANES + u
                sa = sa + s_ref[kidx] * bcast(KK, t, kidx)
            return sa
        return lax.fori_loop(0, n // SUBLANES, body, zeros)

    def advance(t, sa, look_ahead):
        vv = cur[V, t]

        def body(kq, carry):
            out, sa_next = carry
            for u in range(SUBLANES):
                kidx = kq * SUBLANES + u
                s_new = (s_ref[kidx] * bcast(W, t, kidx) + sa * bcast(A, t, kidx)
                         + vv * bcast(K, t, kidx))
                s_ref[kidx] = s_new
                out = out + s_new * bcast(R, t, kidx)
                if look_ahead:
                    sa_next = sa_next + s_new * bcast(KK, t + 1, kidx)
            return out, sa_next
        out, sa_next = lax.fori_loop(0, n // SUBLANES, body, (zeros, zeros))

        mu = jnp.mean(out, axis=0, keepdims=True)
        dlt = out - mu
        var = jnp.mean(dlt * dlt, axis=0, keepdims=True)
        o_n = dlt * lax.rsqrt(var + GN_EPS) * lng_ref[...] + lnb_ref[...]
        bonus = jnp.sum(cur[R, t] * cur[K, t] * rk_ref[...], axis=0, keepdims=True) * vv
        o_buf[t] = o_n + bonus
        return sa_next

    for q, ref in enumerate(srcs):
        if time_on_lanes:
            _block_transpose(ref, cur.at[q], tmp_buf)
        else:
            for t in range(steps):
                cur[q, t] = ref[0, :, t * LANES:(t + 1) * LANES]

    sa = state_dot(0)
    if steps > 1:
        sa = lax.fori_loop(0, steps - 1, lambda t, s: advance(t, s, True), sa)
    advance(steps - 1, sa, False)

    if time_on_lanes:
        _block_transpose(o_buf, o_ref, tmp_buf)
    else:
        for t in range(steps):
            o_ref[0, :, t * LANES:(t + 1) * LANES] = o_buf[t]


def _wkv_scan(streams, col0, n_steps, bn, s0, lng, lnb, rk):
    d = streams[0].shape[0]
    n = HEAD_DIM
    heads = d // n
    time_on_lanes = bn * heads == LANES
    if time_on_lanes:
        steps = LANES // bn
        assert n_steps % steps == 0 and col0 % LANES == 0
        grid = (1, n_steps // steps)
        blk = (heads, n, LANES)
        c0 = col0 // LANES
        in_map = lambda j, t: (0, 0, c0 + t)
        out_map = lambda j, t: (0, 0, t)
    else:
        steps = n_steps
        assert bn == LANES and col0 % (steps * LANES) == 0
        grid = (heads, 1)
        blk = (1, n, steps * LANES)
        c0 = col0 // (steps * LANES)
        in_map = lambda j, t: (j, 0, c0)
        out_map = lambda j, t: (j, 0, 0)
    per_lane = lambda j, t: (0, 0, j)
    par = lambda j, t: (0, j)
    views = [s.reshape(heads, n, s.shape[1]) for s in streams]
    o, s_fin = pl.pallas_call(
        functools.partial(_wkv_kernel, steps=steps, time_on_lanes=time_on_lanes),
        grid=grid,
        in_specs=([pl.BlockSpec(blk, in_map) for _ in range(6)]
                  + [pl.BlockSpec((n, n, LANES), per_lane)]
                  + [pl.BlockSpec((n, LANES), par) for _ in range(3)]),
        out_specs=[pl.BlockSpec(blk, out_map),
                   pl.BlockSpec((n, n, LANES), per_lane)],
        out_shape=(jax.ShapeDtypeStruct((heads, n, n_steps * bn), f32),
                   jax.ShapeDtypeStruct((n, n, heads * bn), f32)),
        scratch_shapes=[pltpu.VMEM((6, steps, n, LANES), f32), pltpu.VMEM((steps, n, LANES), f32),
                        pltpu.VMEM((steps, n, LANES), f32)],
        compiler_params=_cparams(2),
        name="wkv_scan",
    )(*views, s0, lng, lnb, rk)
    return o.reshape(d, n_steps * bn), s_fin


def _reorder_plan(shape_p, shape_s):
    bp, lp, d = shape_p
    bs, ls, _ = shape_s
    rows = bs * ls
    assert rows % bp == 0 and (bp * lp) % rows == 0
    return bp, bs, ls, d, rows, rows // bp, (bp * lp) // rows


def _to_time_major_kernel(xp_ref, xs_ref, o_ref, *, n_pblocks):
    bp, tb, _ = xp_ref.shape
    bs, ls, _ = xs_ref.shape

    @pl.when(pl.program_id(0) < n_pblocks)
    def _():
        for t in range(tb):
            o_ref[t * bp:(t + 1) * bp, :] = xp_ref[:, t, :]

    @pl.when(pl.program_id(0) >= n_pblocks)
    def _():
        for t in range(ls):
            o_ref[t * bs:(t + 1) * bs, :] = xs_ref[:, t, :]


def _to_time_major(x_prompt, x_sample):
    bp, bs, ls, d, rows, tb, n_pblocks = _reorder_plan(x_prompt.shape, x_sample.shape)
    return pl.pallas_call(
        functools.partial(_to_time_major_kernel, n_pblocks=n_pblocks),
        grid=(n_pblocks + 1,),
        in_specs=[pl.BlockSpec((bp, tb, d), lambda i: (0, jnp.minimum(i, n_pblocks - 1), 0)),
                  pl.BlockSpec((bs, ls, d), lambda i: (0, 0, 0))],
        out_specs=pl.BlockSpec((rows, d), lambda i: (i, 0)),
        out_shape=jax.ShapeDtypeStruct(((n_pblocks + 1) * rows, d), x_prompt.dtype),
        compiler_params=_cparams(1),
        name="to_time_major",
    )(x_prompt, x_sample)


def _to_batch_major_kernel(x_ref, yp_ref, ys_ref, *, n_pblocks):
    bp, tb, _ = yp_ref.shape
    bs, ls, _ = ys_ref.shape

    @pl.when(pl.program_id(0) < n_pblocks)
    def _():
        for t in range(tb):
            yp_ref[:, t, :] = x_ref[t * bp:(t + 1) * bp, :]

    @pl.when(pl.program_id(0) >= n_pblocks)
    def _():
        for t in range(ls):
            ys_ref[:, t, :] = x_ref[t * bs:(t + 1) * bs, :]


def _to_batch_major(x, shape_p, shape_s):
    bp, bs, ls, d, rows, tb, n_pblocks = _reorder_plan(shape_p, shape_s)
    return pl.pallas_call(
        functools.partial(_to_batch_major_kernel, n_pblocks=n_pblocks),
        grid=(n_pblocks + 1,),
        in_specs=[pl.BlockSpec((rows, d), lambda i: (i, 0))],
        out_specs=[pl.BlockSpec((bp, tb, d), lambda i: (0, jnp.minimum(i, n_pblocks - 1), 0)),
                   pl.BlockSpec((bs, ls, d), lambda i: (0, 0, 0))],
        out_shape=(jax.ShapeDtypeStruct(shape_p, x.dtype), jax.ShapeDtypeStruct(shape_s, x.dtype)),
        compiler_params=_cparams(1),
        name="to_batch_major",
    )(x)


def _lanes_param(p, bn):
    h = p.size // HEAD_DIM
    y = jnp.transpose(p.reshape(h, HEAD_DIM))
    return jnp.repeat(y, bn, axis=1)


def _col_param(p):
    return jnp.broadcast_to(p.reshape(-1, 1), (p.size, LANES))


def _pad_cols(w, mult=LANES):
    pad = (-w.shape[1]) % mult
    return jnp.pad(w, ((0, 0), (0, pad)))


def _pad_rows(w, mult=LANES):
    pad = (-w.shape[0]) % mult
    return jnp.pad(w, ((0, pad), (0, 0)))


def kernel(x_prompt, x_sample, state_s5_re, state_s5_im, state_wkv, state_shift, ln_gain, ln_bias,
           s5_lambda_re, s5_lambda_im, s5_log_step, s5_b_re, s5_b_im, s5_c_re, s5_c_im, s5_d,
           s5_w_out, s5_w_gate, rw_mix, rw_w_r, rw_w_k, rw_w_v, rw_w_o, rw_w0, rw_w1, rw_w2,
           rw_a0, rw_a1, rw_a2, rw_v0, rw_v1, rw_v2, rw_g1, rw_g2, rw_k_k, rw_k_a, rw_r_k,
           rw_lnx_g, rw_lnx_b, w_router, router_bias, moe_w_gate, moe_w_up, moe_w_down):
    bp, lp, d = x_prompt.shape
    bs, ls, _ = x_sample.shape
    tp, ts = bp * lp, bs * ls
    depth = ln_gain.shape[0]
    alpha = (2.0 * depth) ** 0.25
    n_groups = s5_lambda_re.shape[1]
    n_state = s5_lambda_re.shape[2]
    half = n_groups * n_state
    heads = d // HEAD_DIM

    x = _to_time_major(x_prompt, x_sample)

    wr_t = jnp.transpose(w_router)
    wr_hi = wr_t.astype(bf16)
    wr_lo = (wr_t - wr_hi.astype(f32)).astype(bf16)
    rbias = router_bias.reshape(N_EXPERTS, 1)
    tri = jnp.triu(jnp.ones((TOK_TILE, TOK_TILE), bf16), k=1)
    router = (wr_hi, wr_lo, rbias, tri)

    new_re_p, new_im_p, new_re_s, new_im_s = [], [], [], []
    new_wkv_p, new_wkv_s, new_shift_p, new_shift_s = [], [], [], []
    v_first = None
    groups_per_slab = MXU_DIM // S5_GROUP

    for i in range(depth):
        j = i // 2
        if i % 2 == 0:
            a_re, a_im, bb_re, bb_im = _s5_discretise(s5_lambda_re[j], s5_lambda_im[j], s5_log_step[j],
                                                      s5_b_re[j], s5_b_im[j])
            a_re8 = jnp.broadcast_to(a_re.reshape(1, half), (SUBLANES, half))
            a_im8 = jnp.broadcast_to(a_im.reshape(1, half), (SUBLANES, half))
            n_slab = n_groups // groups_per_slab

            def pack_b(bb):
                w = jnp.transpose(bb, (1, 0, 2)).reshape(n_slab, groups_per_slab, S5_GROUP, n_state)
                return _block_diag(w, groups_per_slab)
            wb = jnp.concatenate([pack_b(bb_re), pack_b(bb_im)], axis=2).astype(bf16)

            def pack_c(c):
                w = jnp.transpose(c, (0, 2, 1)).reshape(n_slab, groups_per_slab, n_state, S5_GROUP)
                return _block_diag(w, groups_per_slab).astype(bf16)
            wc_re, wc_im = pack_c(s5_c_re[j]), pack_c(s5_c_im[j])
            d_skip = s5_d[j].reshape(1, d)

            zero_h = jnp.zeros((bp, half), f32)
            z_p, hre_p, him_p = _s5_scan(x, 0, lp, bp, zero_h, zero_h, a_re8, a_im8,
                                         wb, wc_re, wc_im, d_skip)
            z_s, hre_s, him_s = _s5_scan(x, tp, ls, bs, state_s5_re[j].reshape(bs, half),
                                         state_s5_im[j].reshape(bs, half), a_re8, a_im8,
                                         wb, wc_re, wc_im, d_skip)
            new_re_p.append(hre_p.reshape(bp, n_groups, n_state))
            new_im_p.append(him_p.reshape(bp, n_groups, n_state))
            new_re_s.append(hre_s.reshape(bs, n_groups, n_state))
            new_im_s.append(him_s.reshape(bs, n_groups, n_state))
            x1, ri, rf, cnt = _post_mixer(
                _s5_out_kernel, "s5_out", z_p, z_s, [x],
                [s5_w_out[j].astype(bf16), s5_w_gate[j].astype(bf16)],
                ln_gain[i, 0].reshape(1, d), ln_bias[i, 0].reshape(1, d), router, alpha)
        else:
            new_shift_p.append(x[tp - bp:tp])
            new_shift_s.append(x[tp + ts - bs:tp + ts])
            wt = lambda w: jnp.transpose(w).astype(bf16)
            weights = [rw_mix[j],
                       wt(rw_w_r[j]), wt(rw_w_k[j]), wt(rw_w_v[j]),
                       _col_param(rw_w0[j]), wt(_pad_cols(rw_w1[j])), wt(_pad_rows(rw_w2[j])),
                       _col_param(rw_a0[j]), wt(_pad_cols(rw_a1[j])), wt(_pad_rows(rw_a2[j])),
                       wt(_pad_cols(rw_g1[j], MXU_DIM)), wt(_pad_rows(rw_g2[j], MXU_DIM)),
                       _col_param(rw_k_k[j]), _col_param(rw_k_a[j])]
            if j > 0:
                weights += [_col_param(rw_v0[j - 1]), wt(_pad_cols(rw_v1[j - 1])),
                            wt(_pad_rows(rw_v2[j - 1]))]
            r, w, k, v, nkk, b, g = _rw_proj(x, tp, bp, bs, state_shift[j],
                                            v_first if j > 0 else None, weights)
            if j == 0:
                v_first = v

            def run_group(col0, n_steps, bn, s0):
                o, s_fin = _wkv_scan((r, w, k, v, nkk, b), col0, n_steps, bn, s0,
                                     _lanes_param(rw_lnx_g[j], bn), _lanes_param(rw_lnx_b[j], bn),
                                     _lanes_param(rw_r_k[j], bn))
                s_fin = jnp.transpose(s_fin.reshape(HEAD_DIM, HEAD_DIM, heads, bn), (3, 2, 1, 0))
                return o, s_fin

            o_p, s_p = run_group(0, lp, bp, jnp.zeros((HEAD_DIM, HEAD_DIM, heads * bp), f32))
            s0_s = jnp.transpose(state_wkv[j], (3, 2, 1, 0)).reshape(HEAD_DIM, HEAD_DIM, heads * bs)
            o_s, s_s = run_group(tp, ls, bs, s0_s)
            new_wkv_p.append(s_p)
            new_wkv_s.append(s_s)
            x1, ri, rf, cnt = _post_mixer(
                _rw_out_kernel, "rwkv_out", o_p, o_s, [g, x], [wt(rw_w_o[j])],
                ln_gain[i, 0].reshape(1, d), ln_bias[i, 0].reshape(1, d), router, alpha,
                channel_major=True)

        x = _moe(x1, ri, rf, cnt, i, moe_w_gate, moe_w_up, moe_w_down,
                 ln_gain[i, 1].reshape(1, d), ln_bias[i, 1].reshape(1, d), alpha)

    y_prompt, y_sample = _to_batch_major(x, x_prompt.shape, x_sample.shape)
    return (y_prompt, y_sample,
            jnp.stack(new_re_p), jnp.stack(new_im_p), jnp.stack(new_wkv_p), jnp.stack(new_shift_p),
            jnp.stack(new_re_s), jnp.stack(new_im_s), jnp.stack(new_wkv_s), jnp.stack(new_shift_s))
```
